```python
import math
import jax
import jax.numpy as jnp
from jax import lax
import numpy as np

D_MODEL = 4096
BATCH = 16
SEQ = 256
DEPTH = 2
DEC_BATCH = 2
DEC_SEQ = 4096
PAST_LEN = 512

GRID_W = 64
ROPE_THETA = 10000.0
EPS = 1e-6
Q_BLOCK = 128
N_MOD = 6

GLA_H = 8
GLA_DK = 64
GLA_DV = 128
GLA_GK_RANK = 16
GLA_GATE_NORM = 16.0
GLA_CHUNK = 64
SSD_H = 8
SSD_P = 128
SSD_N = 128
SSD_G = 2
SSD_CONV = 5
SSD_CHUNK = 128
SSD_INNER = SSD_H * SSD_P
SSD_XBC = SSD_INNER + 2 * SSD_G * SSD_N
MLA_H = 8
MLA_Q_RANK = 768
MLA_KV_RANK = 256
MLA_NOPE = 128
MLA_ROPE = 64
MLA_V = 128
MLA_QK_DIM = MLA_NOPE + MLA_ROPE
DIFF_H = 8
DIFF_DH = 64
DIFF_DV = 2 * DIFF_DH
D_MIX = GLA_H * GLA_DV + SSD_INNER + MLA_H * MLA_V + DIFF_H * DIFF_DV
IN_SPLITS = (GLA_H * GLA_DK, GLA_H * GLA_DK, GLA_H * GLA_DV, GLA_H * GLA_DV, 2 * GLA_GK_RANK,
             SSD_INNER, SSD_XBC, 2 * SSD_H,
             MLA_Q_RANK, MLA_KV_RANK, MLA_ROPE,
             DIFF_H * 2 * DIFF_DH, DIFF_H * 2 * DIFF_DH, DIFF_H * DIFF_DV)
D_IN = sum(IN_SPLITS)
PEER_H = 8
PEER_NKEYS = 128
PEER_N = PEER_NKEYS * PEER_NKEYS
PEER_TOPK = 16
PEER_DQ = 256
PEER_BLOCK = 64

kernel_name = 'hybrid_flow_prefix_step'


def rms_norm(x, g):
    xf = x.astype(jnp.float32)
    y = xf * lax.rsqrt(jnp.mean(xf * xf, axis=-1, keepdims=True) + EPS)
    return (y * g.astype(jnp.float32)).astype(x.dtype)


def modulate(x, shift, scale):
    return x * (1.0 + scale[:, None, :]) + shift[:, None, :]


def flip_t(a):
    return jnp.flip(a, axis=1)


def rotate_half(x):
    x1, x2 = jnp.split(x, 2, axis=-1)
    return jnp.concatenate([-x2, x1], axis=-1)


def axial_rope(x, n_tok):
    rows = n_tok // GRID_W
    row = jnp.repeat(jnp.arange(rows), GRID_W).astype(jnp.float32)
    col = jnp.tile(jnp.arange(GRID_W), rows).astype(jnp.float32)
    quarter = x.shape[-1] // 4
    freqs = ROPE_THETA ** (-jnp.arange(quarter, dtype=jnp.float32) / quarter)

    def rot(xa, pos):
        ang = pos[:, None] * freqs[None, :]
        ang = jnp.concatenate([ang, ang], axis=-1)[None, :, None, :]
        return xa.astype(jnp.float32) * jnp.cos(ang) + rotate_half(xa).astype(jnp.float32) * jnp.sin(ang)

    xr, xc = jnp.split(x, 2, axis=-1)
    return jnp.concatenate([rot(xr, row), rot(xc, col)], axis=-1).astype(x.dtype)


def rope_tail(x, n_rope, n_tok):
    return jnp.concatenate([x[..., :-n_rope], axial_rope(x[..., -n_rope:], n_tok)], axis=-1)


def block_attention(q, k, v, scale):
    b, tq, h, d = q.shape
    nb = tq // Q_BLOCK
    qb = q.reshape(b, nb, Q_BLOCK, h, d).transpose(1, 0, 2, 3, 4)

    def one(qi):
        s = jnp.einsum('bqhd,bkhd->bhqk', qi, k).astype(jnp.float32) * scale
        pr = jax.nn.softmax(s, axis=-1).astype(v.dtype)
        return jnp.einsum('bhqk,bkhe->bqhe', pr, v)

    out = lax.map(one, qb)
    return out.transpose(1, 0, 2, 3, 4).reshape(b, tq, h, v.shape[-1])


def centred_dwconv(x, w, bias):
    pad = (SSD_CONV - 1) // 2
    y = lax.conv_general_dilated(x, w[:, None, :].astype(x.dtype), window_strides=(1,), padding=[(pad, pad)],
                                 dimension_numbers=('NWC', 'WIO', 'NWC'), feature_group_count=x.shape[-1])
    return y + bias


def gla_chunk_scan(q, k, v, log_a, s0):
    b, t, h, _ = q.shape
    dv = v.shape[-1]
    nc = t // GLA_CHUNK

    def to_chunks(a):
        return a.astype(jnp.float32).reshape(b, nc, GLA_CHUNK, h, a.shape[-1]).transpose(1, 0, 3, 2, 4)

    mask = jnp.tril(jnp.ones((GLA_CHUNK, GLA_CHUNK), bool))[None, None, :, :, None]

    def step(s_prev, inp):
        qc, kc, vc, ac = inp
        cum = jnp.cumsum(ac, axis=2)
        o_inter = jnp.einsum('bhtd,bhde->bhte', qc * jnp.exp(cum), s_prev)
        rel = jnp.where(mask, cum[:, :, :, None, :] - cum[:, :, None, :, :], -jnp.inf)
        att = jnp.einsum('bhtd,bhtsd,bhsd->bhts', qc, jnp.exp(rel), kc)
        o_intra = jnp.einsum('bhts,bhse->bhte', att, vc)
        last = cum[:, :, -1, :]
        s_new = jnp.exp(last)[..., None] * s_prev + jnp.einsum('bhsd,bhse->bhde', kc * jnp.exp(last[:, :, None, :] - cum), vc)
        return s_new, o_inter + o_intra

    s_fin, o = lax.scan(step, s0.astype(jnp.float32), (to_chunks(q), to_chunks(k), to_chunks(v), to_chunks(log_a)))
    return o.transpose(1, 0, 3, 2, 4).reshape(b, t, h, dv), s_fin


def ssd_chunk_scan(x, dt, a, bm, cm, h0):
    b, t, h, _ = x.shape
    nc = t // SSD_CHUNK

    def ch4(z):
        return z.astype(jnp.float32).reshape(b, nc, SSD_CHUNK, h, z.shape[-1]).transpose(1, 0, 3, 2, 4)

    def ch3(z):
        return z.astype(jnp.float32).reshape(b, nc, SSD_CHUNK, h).transpose(1, 0, 3, 2)

    mask = jnp.tril(jnp.ones((SSD_CHUNK, SSD_CHUNK), bool))[None, None]
    la = dt.astype(jnp.float32) * a.astype(jnp.float32)

    def step(hs, inp):
        xc, dtc, lac, bc, cc = inp
        cum = jnp.cumsum(lac, axis=-1)
        seg = jnp.exp(jnp.where(mask, cum[..., :, None] - cum[..., None, :], -jnp.inf))
        scores = jnp.einsum('bhtn,bhsn->bhts', cc, bc) * seg * dtc[:, :, None, :]
        y_intra = jnp.einsum('bhts,bhsp->bhtp', scores, xc)
        y_inter = jnp.einsum('bhtn,bhpn->bhtp', cc, hs) * jnp.exp(cum)[..., None]
        wgt = jnp.exp(cum[..., -1:] - cum) * dtc
        h_new = jnp.exp(cum[..., -1])[..., None, None] * hs + jnp.einsum('bhs,bhsn,bhsp->bhpn', wgt, bc, xc)
        return h_new, y_intra + y_inter

    h_fin, y = lax.scan(step, h0.astype(jnp.float32), (ch4(x), ch3(dt), ch3(la), ch4(bm), ch4(cm)))
    return y.transpose(1, 0, 3, 2, 4).reshape(b, t, h, x.shape[-1]), h_fin


def split_cols(z):
    offs = np.cumsum(np.array(IN_SPLITS))[:-1].tolist()
    return jnp.split(z, offs, axis=-1)


def mla_keys(ckv, kpe, p):
    b, t, _ = ckv.shape
    kv = (ckv @ p['mla_w_kvb']).reshape(b, t, MLA_H, MLA_NOPE + MLA_V)
    k_nope, v = kv[..., :MLA_NOPE], kv[..., MLA_NOPE:]
    k_pe = jnp.broadcast_to(kpe[:, :, None, :], (b, t, MLA_H, MLA_ROPE))
    k = rms_norm(jnp.concatenate([k_nope, k_pe], axis=-1), p['mla_k_norm_g'])
    return k, v


def token_mixers(u, p, layer_idx, ctx):
    b, t, _ = u.shape
    latent = ctx is not None
    (a_q, a_k, a_v, a_g, a_gk, b_z, b_xbc, b_dt, c_qa, c_kva, c_kpe, d_q, d_k, d_v) = split_cols(u @ p['w_in'])

    gq = a_q.reshape(b, t, GLA_H, GLA_DK) * (GLA_DK ** -0.5)
    gk = a_k.reshape(b, t, GLA_H, GLA_DK)
    gv = a_v.reshape(b, t, GLA_H, GLA_DV)
    gate_in = jnp.einsum('btdr,drk->btdk', a_gk.reshape(b, t, 2, GLA_GK_RANK), p['gla_w_gk']) + p['gla_b_gk']
    log_a = (jax.nn.log_sigmoid(gate_in.astype(jnp.float32)) / GLA_GATE_NORM).reshape(b, t, 2, GLA_H, GLA_DK)
    if latent:
        sg_f, sg_b = ctx['gla'][:, 0], ctx['gla'][:, 1]
    else:
        sg_f = jnp.zeros((b, GLA_H, GLA_DK, GLA_DV), jnp.float32)
        sg_b = sg_f
    o_f, fin_gf = gla_chunk_scan(gq, gk, gv, log_a[:, :, 0], sg_f)
    o_b, fin_gb = gla_chunk_scan(flip_t(gq), flip_t(gk), flip_t(gv), flip_t(log_a[:, :, 1]), sg_b)
    o_gla = (o_f + flip_t(o_b)).astype(u.dtype)
    out_a = (rms_norm(o_gla, p['gla_norm_g']) * jax.nn.silu(a_g.reshape(b, t, GLA_H, GLA_DV))).reshape(b, t, -1)

    xbc = jax.nn.silu(centred_dwconv(b_xbc, p['ssd_conv_w'], p['ssd_conv_b']))
    sx = xbc[..., :SSD_INNER].reshape(b, t, SSD_H, SSD_P)
    rep = SSD_H // SSD_G
    sb = jnp.repeat(xbc[..., SSD_INNER:SSD_INNER + SSD_G * SSD_N].reshape(b, t, SSD_G, SSD_N), rep, axis=2)
    sc = jnp.repeat(xbc[..., SSD_INNER + SSD_G * SSD_N:].reshape(b, t, SSD_G, SSD_N), rep, axis=2)
    dt = jax.nn.softplus(b_dt.reshape(b, t, 2, SSD_H).astype(jnp.float32) + p['ssd_dt_bias'].astype(jnp.float32))
    a_neg = -jnp.exp(p['ssd_a_log'].astype(jnp.float32))
    if latent:
        hs_f, hs_b = ctx['ssd'][:, 0], ctx['ssd'][:, 1]
    else:
        hs_f = jnp.zeros((b, SSD_H, SSD_P, SSD_N), jnp.float32)
        hs_b = hs_f
    y_f, fin_sf = ssd_chunk_scan(sx, dt[:, :, 0], a_neg[0], sb, sc, hs_f)
    y_b, fin_sb = ssd_chunk_scan(flip_t(sx), flip_t(dt[:, :, 1]), a_neg[1], flip_t(sb), flip_t(sc), hs_b)
    y_ssd = (y_f + flip_t(y_b) + p['ssd_d'].astype(jnp.float32)[:, None] * sx.astype(jnp.float32)).astype(u.dtype)
    y_ssd = y_ssd * jax.nn.silu(b_z.reshape(b, t, SSD_H, SSD_P))
    out_b = rms_norm(y_ssd.reshape(b, t, SSD_INNER), p['ssd_norm_g'])

    mq = (rms_norm(c_qa, p['mla_qa_norm_g']) @ p['mla_w_qb']).reshape(b, t, MLA_H, MLA_QK_DIM)
    mq = rms_norm(mq, p['mla_q_norm_g'])
    ckv = rms_norm(c_kva, p['mla_kva_norm_g'])
    kpe = c_kpe
    mk, mv = mla_keys(ckv, kpe, p)
    if latent:
        mq = rope_tail(mq, MLA_ROPE, t)
        mk = rope_tail(mk, MLA_ROPE, t)
        ck, cv = mla_keys(ctx['mla_ckv'], ctx['mla_kpe'], p)
        mk = jnp.concatenate([ck, mk], axis=1)
        mv = jnp.concatenate([cv, mv], axis=1)
    out_c = block_attention(mq, mk, mv, MLA_QK_DIM ** -0.5).reshape(b, t, -1)

    dq = rms_norm(d_q.reshape(b, t, DIFF_H, 2, DIFF_DH), p['diff_q_norm_g'])
    dk = rms_norm(d_k.reshape(b, t, DIFF_H, 2, DIFF_DH), p['diff_k_norm_g'])
    dv = d_v.reshape(b, t, DIFF_H, DIFF_DV)
    ctx_dk, ctx_dv = dk, dv
    if latent:
        dq = axial_rope(dq.reshape(b, t, DIFF_H * 2, DIFF_DH), t).reshape(b, t, DIFF_H, 2, DIFF_DH)
        dk = axial_rope(dk.reshape(b, t, DIFF_H * 2, DIFF_DH), t).reshape(b, t, DIFF_H, 2, DIFF_DH)
        dk = jnp.concatenate([ctx['diff_k'], dk], axis=1)
        dv = jnp.concatenate([ctx['diff_v'], dv], axis=1)
    lam_init = 0.8 - 0.6 * math.exp(-0.3 * layer_idx)
    lam = p['diff_lambda'].astype(jnp.float32)
    lam_full = jnp.exp(jnp.sum(lam[0] * lam[1])) - jnp.exp(jnp.sum(lam[2] * lam[3])) + lam_init
    o1 = block_attention(dq[:, :, :, 0], dk[:, :, :, 0], dv, DIFF_DH ** -0.5)
    o2 = block_attention(dq[:, :, :, 1], dk[:, :, :, 1], dv, DIFF_DH ** -0.5)
    od = o1.astype(jnp.float32) - lam_full * o2.astype(jnp.float32)
    out_d = (rms_norm(od, p['diff_subln_g']) * (1.0 - lam_init)).astype(u.dtype).reshape(b, t, -1)

    y = jnp.concatenate([out_a, out_b, out_c, out_d], axis=-1) @ p['w_out']
    if latent:
        return y, None
    st = (ckv, kpe, ctx_dk, ctx_dv, jnp.stack([fin_gf, fin_gb], axis=1), jnp.stack([fin_sf, fin_sb], axis=1))
    return y, st


def peer(u, p):
    b, t, d = u.shape
    n = b * t
    x = u.reshape(n, d)
    q = (x @ p['peer_w_q']).reshape(n, PEER_H, 2, PEER_DQ // 2)
    s = jnp.einsum('nhpd,hpkd->nhpk', q, p['peer_sub_keys']).astype(jnp.float32)
    s1, i1 = lax.top_k(s[:, :, 0], PEER_TOPK)
    s2, i2 = lax.top_k(s[:, :, 1], PEER_TOPK)
    cand_s = (s1[..., :, None] + s2[..., None, :]).reshape(n, PEER_H, PEER_TOPK * PEER_TOPK)
    cand_i = (i1[..., :, None] * PEER_NKEYS + i2[..., None, :]).reshape(n, PEER_H, PEER_TOPK * PEER_TOPK)
    top_s, pos = lax.top_k(cand_s, PEER_TOPK)
    idx = jnp.take_along_axis(cand_i, pos, axis=-1)
    g = jax.nn.softmax(top_s, axis=-1)
    nb = n // PEER_BLOCK

    def one(args):
        xb, ib, gb = args
        act = jax.nn.gelu(jnp.einsum('pd,ped->pe', xb, p['peer_u'][ib]).astype(jnp.float32))
        return jnp.einsum('pe,ped->pd', (gb * act).astype(xb.dtype), p['peer_v'][ib])

    out = lax.map(one, (x.reshape(nb, PEER_BLOCK, d), idx.reshape(nb, PEER_BLOCK, -1), g.reshape(nb, PEER_BLOCK, -1)))
    return out.reshape(b, t, d)


def modulation(cvec, p):
    m = jax.nn.silu(cvec) @ p['w_ada'] + p['b_ada']
    return m.reshape(m.shape[0], N_MOD, D_MODEL)


def trunk_layer(x, mod, p, layer_idx, ctx):
    u = modulate(rms_norm(x, p['norm1_g']), mod[:, 0], mod[:, 1])
    mix, st = token_mixers(u, p, layer_idx, ctx)
    h = x + mod[:, 2][:, None, :] * mix
    u2 = modulate(rms_norm(h, p['norm2_g']), mod[:, 3], mod[:, 4])
    return h + mod[:, 5][:, None, :] * peer(u2, p), st


def setup_inputs(seed: int = 0) -> dict:
    key = jax.random.key(seed)
    it = iter(jax.random.split(key, 48))

    def nrm(shape, s):
        return s * jax.random.normal(next(it), shape, jnp.float32)

    def gain(shape):
        return 1.0 + 0.05 * jax.random.normal(next(it), shape, jnp.float32)

    dt0 = jnp.exp(jax.random.uniform(next(it), (DEPTH, 2, SSD_H), jnp.float32, math.log(1e-3), math.log(1e-1)))
    return {
        'x_prompt': nrm((BATCH, SEQ, D_MODEL), 1.0),
        'x_sample': nrm((DEC_BATCH, DEC_SEQ, D_MODEL), 1.0),
        'cache_mla_ckv': nrm((DEC_BATCH, DEPTH, PAST_LEN, MLA_KV_RANK), 1.0),
        'cache_mla_kpe': nrm((DEC_BATCH, DEPTH, PAST_LEN, MLA_ROPE), 1.0),
        'cache_diff_k': nrm((DEC_BATCH, DEPTH, PAST_LEN, DIFF_H, 2, DIFF_DH), 1.0),
        'cache_diff_v': nrm((DEC_BATCH, DEPTH, PAST_LEN, DIFF_H, DIFF_DV), 1.0),
        'state_gla': nrm((DEC_BATCH, DEPTH, 2, GLA_H, GLA_DK, GLA_DV), 0.5),
        'state_ssd': nrm((DEC_BATCH, DEPTH, 2, SSD_H, SSD_P, SSD_N), 0.5),
        'c': nrm((DEC_BATCH, D_MODEL), 1.0),
        'c_ctx': nrm((D_MODEL,), 1.0),
        'norm1_g': gain((DEPTH, D_MODEL)),
        'norm2_g': gain((DEPTH, D_MODEL)),
        'w_ada': nrm((DEPTH, D_MODEL, N_MOD * D_MODEL), 0.5 * D_MODEL ** -0.5),
        'b_ada': nrm((DEPTH, N_MOD * D_MODEL), 0.02),
        'w_in': nrm((DEPTH, D_MODEL, D_IN), D_MODEL ** -0.5),
        'w_out': nrm((DEPTH, D_MIX, D_MODEL), D_MIX ** -0.5),
        'gla_w_gk': nrm((DEPTH, 2, GLA_GK_RANK, GLA_H * GLA_DK), GLA_GK_RANK ** -0.5),
        'gla_b_gk': nrm((DEPTH, 2, GLA_H * GLA_DK), 0.1),
        'gla_norm_g': gain((DEPTH, GLA_DV)),
        'ssd_conv_w': nrm((DEPTH, SSD_CONV, SSD_XBC), SSD_CONV ** -0.5),
        'ssd_conv_b': nrm((DEPTH, SSD_XBC), 0.02),
        'ssd_a_log': jnp.log(jax.random.uniform(next(it), (DEPTH, 2, SSD_H), jnp.float32, 1.0, 16.0)),
        'ssd_dt_bias': dt0 + jnp.log(-jnp.expm1(-dt0)),
        'ssd_d': gain((DEPTH, SSD_H)),
        'ssd_norm_g': gain((DEPTH, SSD_INNER)),
        'mla_qa_norm_g': gain((DEPTH, MLA_Q_RANK)),
        'mla_w_qb': nrm((DEPTH, MLA_Q_RANK, MLA_H * MLA_QK_DIM), MLA_Q_RANK ** -0.5),
        'mla_kva_norm_g': gain((DEPTH, MLA_KV_RANK)),
        'mla_w_kvb': nrm((DEPTH, MLA_KV_RANK, MLA_H * (MLA_NOPE + MLA_V)), MLA_KV_RANK ** -0.5),
        'mla_q_norm_g': gain((DEPTH, MLA_QK_DIM)),
        'mla_k_norm_g': gain((DEPTH, MLA_QK_DIM)),
        'diff_q_norm_g': gain((DEPTH, DIFF_DH)),
        'diff_k_norm_g': gain((DEPTH, DIFF_DH)),
        'diff_lambda': nrm((DEPTH, 4, DIFF_DH), 0.1),
        'diff_subln_g': gain((DEPTH, DIFF_DV)),
        'peer_w_q': nrm((DEPTH, D_MODEL, PEER_H * PEER_DQ), D_MODEL ** -0.5),
        'peer_sub_keys': nrm((DEPTH, PEER_H, 2, PEER_NKEYS, PEER_DQ // 2), (PEER_DQ // 2) ** -0.5),
        'peer_u': nrm((DEPTH, PEER_N, D_MODEL), D_MODEL ** -0.5),
        'peer_v': nrm((DEPTH, PEER_N, D_MODEL), 0.5),
    }


def reference(x_prompt, x_sample, cache_mla_ckv, cache_mla_kpe, cache_diff_k, cache_diff_v, state_gla, state_ssd,
              c, c_ctx, norm1_g, norm2_g, w_ada, b_ada, w_in, w_out, gla_w_gk, gla_b_gk, gla_norm_g,
              ssd_conv_w, ssd_conv_b, ssd_a_log, ssd_dt_bias, ssd_d, ssd_norm_g,
              mla_qa_norm_g, mla_w_qb, mla_kva_norm_g, mla_w_kvb, mla_q_norm_g, mla_k_norm_g,
              diff_q_norm_g, diff_k_norm_g, diff_lambda, diff_subln_g,
              peer_w_q, peer_sub_keys, peer_u, peer_v):
    y_p = x_prompt
    y_s = x_sample
    ckv_l, kpe_l, dk_l, dv_l, gla_l, ssd_l = [], [], [], [], [], []
    for l in range(DEPTH):
        p = dict(norm1_g=norm1_g[l], norm2_g=norm2_g[l], w_ada=w_ada[l], b_ada=b_ada[l], w_in=w_in[l], w_out=w_out[l],
                 gla_w_gk=gla_w_gk[l], gla_b_gk=gla_b_gk[l], gla_norm_g=gla_norm_g[l],
                 ssd_conv_w=ssd_conv_w[l], ssd_conv_b=ssd_conv_b[l], ssd_a_log=ssd_a_log[l],
                 ssd_dt_bias=ssd_dt_bias[l], ssd_d=ssd_d[l], ssd_norm_g=ssd_norm_g[l],
                 mla_qa_norm_g=mla_qa_norm_g[l], mla_w_qb=mla_w_qb[l], mla_kva_norm_g=mla_kva_norm_g[l],
                 mla_w_kvb=mla_w_kvb[l], mla_q_norm_g=mla_q_norm_g[l], mla_k_norm_g=mla_k_norm_g[l],
                 diff_q_norm_g=diff_q_norm_g[l], diff_k_norm_g=diff_k_norm_g[l], diff_lambda=diff_lambda[l],
                 diff_subln_g=diff_subln_g[l], peer_w_q=peer_w_q[l], peer_sub_keys=peer_sub_keys[l],
                 peer_u=peer_u[l], peer_v=peer_v[l])
        mod_ctx = modulation(c_ctx[None, :], p)
        mod_lat = modulation(c, p)
        ctx_l = dict(mla_ckv=cache_mla_ckv[:, l], mla_kpe=cache_mla_kpe[:, l], diff_k=cache_diff_k[:, l],
                     diff_v=cache_diff_v[:, l], gla=state_gla[:, l], ssd=state_ssd[:, l])
        y_p, st = trunk_layer(y_p, mod_ctx, p, l, None)
        y_s, _ = trunk_layer(y_s, mod_lat, p, l, ctx_l)
        ckv_l.append(st[0])
        kpe_l.append(st[1])
        dk_l.append(st[2])
        dv_l.append(st[3])
        gla_l.append(st[4])
        ssd_l.append(st[5])
    new_mla_ckv = jnp.stack(ckv_l, axis=1)
    new_mla_kpe = jnp.stack(kpe_l, axis=1)
    new_diff_k = jnp.stack(dk_l, axis=1)
    new_diff_v = jnp.stack(dv_l, axis=1)
    new_state_gla = jnp.stack(gla_l, axis=1)
    new_state_ssd = jnp.stack(ssd_l, axis=1)
    return (y_p, y_s, new_mla_ckv, new_mla_kpe, new_diff_k, new_diff_v, new_state_gla, new_state_ssd)
```

```python
import functools
import math

import jax
import jax.numpy as jnp
import numpy as np
from jax import lax
from jax.experimental import pallas as pl
from jax.experimental.pallas import tpu as pltpu

F32 = jnp.float32
BF = jnp.bfloat16
HIGHEST = lax.Precision.HIGHEST

D = 4096
EPS = 1e-6
N_CTX = 4096
N_LAT = 8192
N_TOK = N_CTX + N_LAT
MOD_ROWS = 4096
T_CTX, B_CTX = 256, 16
T_LAT, B_LAT = 4096, 2
PAST = 512
GRID_W = 64
ROPE_THETA = 10000.0

V7X_VMEM_LIMIT = 56 * 1024 * 1024

Z_AV, Z_AG, Z_BZ, Z_DQ, Z_DK, Z_DV = 0, 1024, 2048, 3072, 4096, 5120
Z_XBC, Z_AQ, Z_AK, Z_KVA, Z_SM, Z_QA = 6144, 7680, 8192, 8704, 8960, 9216
Z_W = 9984
SM_GK, SM_DT, SM_KPE = 0, 32, 64

NT = (((1,), (1,)), ((), ()))
TN = (((0,), (0,)), ((), ()))


def _cp(*sem):
    return pltpu.CompilerParams(dimension_semantics=sem, vmem_limit_bytes=V7X_VMEM_LIMIT)


def _sigmoid(x):
    return 1.0 / (1.0 + jnp.exp(-x))


def _silu(x):
    return x * _sigmoid(x)


def _softplus(x):
    return jnp.maximum(x, 0.0) + jnp.log1p(jnp.exp(-jnp.abs(x)))


def _log_sigmoid(x):
    return jnp.minimum(x, 0.0) - jnp.log1p(jnp.exp(-jnp.abs(x)))


def _mod_kernel(c_ref, w_ref, b_ref, o_ref):
    a = _silu(c_ref[...])
    o_ref[...] = jnp.dot(a.astype(BF), w_ref[...].astype(BF), preferred_element_type=F32) + b_ref[...]


def _modulation(cv8, w_ada, b_ada, layer):
    tn = 512
    return pl.pallas_call(
        _mod_kernel,
        grid=(6 * D // tn,),
        in_specs=[pl.BlockSpec((8, D), lambda j: (0, 0)),
                  pl.BlockSpec((None, D, tn), lambda j: (layer, 0, j)),
                  pl.BlockSpec((None, 1, tn), lambda j: (layer, 0, j))],
        out_specs=pl.BlockSpec((8, tn), lambda j: (0, j)),
        out_shape=jax.ShapeDtypeStruct((8, 6 * D), F32),
        compiler_params=_cp("arbitrary"),
        name="modulation",
    )(cv8, w_ada, b_ada.reshape(b_ada.shape[0], 1, 6 * D))


def _norm_mm_kernel(x_ref, g_ref, mod_ref, w_ref, o_ref, *rest, shift_row, emit_xn):
    if emit_xn:
        xn_out_ref, xn = rest
    else:
        (xn,) = rest

    @pl.when(pl.program_id(1) == 0)
    def _():
        m = mod_ref[0]
        gain = g_ref[...] * (1.0 + m[shift_row + 1:shift_row + 2, :])
        shift = m[shift_row:shift_row + 1, :]

        def chunk(r, carry):
            rows = pl.ds(pl.multiple_of(r * 64, 64), 64)
            x = x_ref[rows, :]
            u = (x * lax.rsqrt(jnp.mean(x * x, axis=-1, keepdims=True) + EPS) * gain + shift).astype(BF)
            xn[rows, :] = u
            if emit_xn:
                xn_out_ref[rows, :] = u
            return carry

        lax.fori_loop(0, x_ref.shape[0] // 64, chunk, 0)

    o_ref[...] = jnp.dot(xn[...], w_ref[...], preferred_element_type=F32).astype(o_ref.dtype)


def _norm_mm(x, g, mod3, w, *, shift_row, tn, emit_xn, name):
    tm = 512
    n, k = x.shape
    nout = w.shape[1]
    out_shape = [jax.ShapeDtypeStruct((n, nout), BF)]
    out_specs = [pl.BlockSpec((tm, tn), lambda i, j: (i, j))]
    if emit_xn:
        out_shape.append(jax.ShapeDtypeStruct((n, k), BF))
        out_specs.append(pl.BlockSpec((tm, k), lambda i, j: (i, 0)))
    res = pl.pallas_call(
        functools.partial(_norm_mm_kernel, shift_row=shift_row, emit_xn=emit_xn),
        grid=(n // tm, nout // tn),
        in_specs=[pl.BlockSpec((tm, k), lambda i, j: (i, 0)),
                  pl.BlockSpec((1, k), lambda i, j: (0, 0)),
                  pl.BlockSpec((1, 6, k), lambda i, j: (i * tm // MOD_ROWS, 0, 0)),
                  pl.BlockSpec((k, tn), lambda i, j: (0, j))],
        out_specs=out_specs,
        out_shape=out_shape,
        scratch_shapes=[pltpu.VMEM((tm, k), BF)],
        compiler_params=_cp("arbitrary", "arbitrary"),
        name=name,
    )(x, g.reshape(1, k), mod3, w)
    return res


def _out_proj_kernel(a_ref, b_ref, c_ref, d_ref, w_ref, x_ref, mod_ref, o_ref):
    acc = jnp.dot(a_ref[...], w_ref[0:1024, :], preferred_element_type=F32)
    acc += jnp.dot(b_ref[...], w_ref[1024:2048, :], preferred_element_type=F32)
    acc += jnp.dot(c_ref[...], w_ref[2048:3072, :], preferred_element_type=F32)
    acc += jnp.dot(d_ref[...], w_ref[3072:4096, :], preferred_element_type=F32)
    o_ref[...] = x_ref[...] + mod_ref[0][2:3, :] * acc


def _out_proj(mix, w, x, mod3):
    tm, tn = 512, 1024
    n = x.shape[0]
    mspec = pl.BlockSpec((tm, 1024), lambda i, j: (i, 0))
    return pl.pallas_call(
        _out_proj_kernel,
        grid=(n // tm, D // tn),
        in_specs=[mspec, mspec, mspec, mspec,
                  pl.BlockSpec((D, tn), lambda i, j: (0, j)),
                  pl.BlockSpec((tm, tn), lambda i, j: (i, j)),
                  pl.BlockSpec((1, 6, tn), lambda i, j: (i * tm // MOD_ROWS, 0, j))],
        out_specs=pl.BlockSpec((tm, tn), lambda i, j: (i, j)),
        out_shape=jax.ShapeDtypeStruct((n, D), F32),
        compiler_params=_cp("arbitrary", "arbitrary"),
        name="out_proj",
    )(*mix, w, x, mod3)


def _resid_kernel(h_ref, p_ref, mod_ref, o_ref):
    o_ref[...] = h_ref[...] + mod_ref[0][5:6, :] * p_ref[...]


def _resid(h, p, mod3):
    tm = 256
    n = h.shape[0]
    spec = pl.BlockSpec((tm, D), lambda i: (i, 0))
    return pl.pallas_call(
        _resid_kernel,
        grid=(n // tm,),
        in_specs=[spec, spec, pl.BlockSpec((1, 6, D), lambda i: (i * tm // MOD_ROWS, 0, 0))],
        out_specs=spec,
        out_shape=jax.ShapeDtypeStruct((n, D), F32),
        compiler_params=_cp("arbitrary"),
        name="peer_residual",
    )(h, p, mod3)


GLA_BLK = 16


def _gla_kernel(q_ref, k_ref, v_ref, g_ref, sm_ref, wgk_ref, bgk_ref, ng_ref, s0_ref,
                o_ref, sf_ref, oacc, cumf, cumb, st, *, T):
    nb = T // GLA_BLK
    r256 = lax.broadcasted_iota(jnp.int32, (256, 256), 0)
    c256 = lax.broadcasted_iota(jnp.int32, (256, 256), 1)
    same = (r256 // GLA_BLK) == (c256 // GLA_BLK)
    lower = jnp.where(same, jnp.where(r256 >= c256, 1.0, 0.0), 0.0).astype(F32)
    upper = jnp.where(same, jnp.where(r256 <= c256, 1.0, 0.0), 0.0).astype(F32)

    def gate_body(r, carry):
        rows = pl.ds(pl.multiple_of(r * 256, 256), 256)
        smb = sm_ref[rows, :]
        gf = jnp.dot(smb, wgk_ref[0], preferred_element_type=F32) + bgk_ref[0]
        gb = jnp.dot(smb, wgk_ref[1], preferred_element_type=F32) + bgk_ref[1]
        laf = _log_sigmoid(gf) * (1.0 / 16.0)
        lab = _log_sigmoid(gb) * (1.0 / 16.0)
        cumf[rows, :] = jnp.dot(lower, laf, precision=HIGHEST, preferred_element_type=F32)
        cumb[rows, :] = jnp.dot(upper, lab, precision=HIGHEST, preferred_element_type=F32)
        oacc[rows, :] = jnp.zeros((256, 256), F32)
        return carry

    lax.fori_loop(0, T // 256, gate_body, 0)

    for d in range(2):
        for h in range(2):
            st[d, h] = s0_ref[0, d, h].T

    scale = 64.0 ** -0.5
    rowi = lax.broadcasted_iota(jnp.int32, (GLA_BLK, 128), 0)
    lane = lax.broadcasted_iota(jnp.int32, (GLA_BLK, 128), 1)
    lo = lane < 64

    def body(i, carry):
        for d in range(2):
            blk = i if d == 0 else nb - 1 - i
            rows = pl.ds(pl.multiple_of(blk * GLA_BLK, GLA_BLK), GLA_BLK)
            qb = q_ref[rows, :].astype(F32) * scale
            kb = k_ref[rows, :].astype(F32)
            vb = v_ref[rows, :]
            vf = vb.astype(F32)
            cb = cumf[rows, :] if d == 0 else cumb[rows, :]
            edge = cb[GLA_BLK - 1:GLA_BLK, :] if d == 0 else cb[0:1, :]
            qh = qb * jnp.exp(cb)
            kt = kb * jnp.exp(edge - cb)
            dec = jnp.exp(edge)
            od0 = jnp.zeros((GLA_BLK, 128), F32)
            od1 = jnp.zeros((GLA_BLK, 128), F32)
            for s in range(GLA_BLK):
                msk = (rowi >= s) if d == 0 else (rowi <= s)
                w = qb * kb[s:s + 1, :] * jnp.exp(jnp.minimum(cb - cb[s:s + 1, :], 0.0))
                w = jnp.where(msk, w, 0.0)
                a0 = jnp.sum(jnp.where(lo, w, 0.0), axis=1, keepdims=True)
                a1 = jnp.sum(jnp.where(lo, 0.0, w), axis=1, keepdims=True)
                od0 = od0 + a0 * vf[s:s + 1, 0:128]
                od1 = od1 + a1 * vf[s:s + 1, 128:256]
            for h in range(2):
                hs = slice(h * 64, h * 64 + 64)
                s_t = st[d, h]
                o_h = lax.dot_general(qh[:, hs].astype(BF), s_t.astype(BF), NT, preferred_element_type=F32)
                o_h = o_h + (od0 if h == 0 else od1)
                upd = lax.dot_general(vb[:, h * 128:(h + 1) * 128], kt[:, hs].astype(BF), TN,
                                      preferred_element_type=F32)
                st[d, h] = s_t * dec[:, hs] + upd
                oacc[rows, h * 128:(h + 1) * 128] += o_h
        return carry

    lax.fori_loop(0, nb, body, 0)

    for d in range(2):
        for h in range(2):
            sf_ref[0, d, h] = st[d, h].T

    def epi(r, carry):
        rows = pl.ds(pl.multiple_of(r * 256, 256), 256)
        for h in range(2):
            cols = slice(h * 128, (h + 1) * 128)
            o = oacc[rows, cols]
            y = o * lax.rsqrt(jnp.mean(o * o, axis=-1, keepdims=True) + EPS) * ng_ref[...]
            o_ref[rows, cols] = (y * _silu(g_ref[rows, cols].astype(F32))).astype(BF)
        return carry

    lax.fori_loop(0, T // 256, epi, 0)


def _gla(z, wgk_pad, bgk, ng, s0, *, nseq, T, row0):
    return pl.pallas_call(
        functools.partial(_gla_kernel, T=T),
        grid=(nseq, 4),
        in_specs=[pl.BlockSpec((T, 128), lambda s, p: (row0 + s, Z_AQ // 128 + p)),
                  pl.BlockSpec((T, 128), lambda s, p: (row0 + s, Z_AK // 128 + p)),
                  pl.BlockSpec((T, 256), lambda s, p: (row0 + s, Z_AV // 256 + p)),
                  pl.BlockSpec((T, 256), lambda s, p: (row0 + s, Z_AG // 256 + p)),
                  pl.BlockSpec((T, 128), lambda s, p: (row0 + s, Z_SM // 128)),
                  pl.BlockSpec((2, 128, 128), lambda s, p: (0, 0, p)),
                  pl.BlockSpec((2, 1, 128), lambda s, p: (0, 0, p)),
                  pl.BlockSpec((1, 128), lambda s, p: (0, 0)),
                  pl.BlockSpec((1, 2, 2, 64, 128), lambda s, p: (s, 0, p, 0, 0))],
        out_specs=[pl.BlockSpec((T, 256), lambda s, p: (s, p)),
                   pl.BlockSpec((1, 2, 2, 64, 128), lambda s, p: (s, 0, p, 0, 0))],
        out_shape=[jax.ShapeDtypeStruct((nseq * T, 1024), BF),
                   jax.ShapeDtypeStruct((nseq, 2, 8, 64, 128), F32)],
        scratch_shapes=[pltpu.VMEM((T, 256), F32), pltpu.VMEM((T, 128), F32), pltpu.VMEM((T, 128), F32),
                        pltpu.VMEM((2, 2, 128, 64), F32)],
        compiler_params=_cp("arbitrary", "arbitrary"),
        name=f"gla_T{T}",
    )(z, z, z, z, z, wgk_pad, bgk, ng, s0)


def _conv_kernel(x_ref, w_ref, b_ref, o_ref, xp, *, T):
    xp[0:8, :] = jnp.zeros((8, 256), F32)
    xp[8 + T:16 + T, :] = jnp.zeros((8, 256), F32)
    xp[8:8 + T, :] = x_ref[...].astype(F32)
    for r in range(T // 256):
        acc = b_ref[...] + w_ref[0:1, :] * xp[6 + r * 256:6 + (r + 1) * 256, :]
        for kk in range(1, 5):
            acc = acc + w_ref[kk:kk + 1, :] * xp[6 + kk + r * 256:6 + kk + (r + 1) * 256, :]
        o_ref[r * 256:(r + 1) * 256, :] = _silu(acc).astype(BF)


def _conv(z, w, b, *, nseq, T, row0):
    return pl.pallas_call(
        functools.partial(_conv_kernel, T=T),
        grid=(nseq, 6),
        in_specs=[pl.BlockSpec((T, 256), lambda s, j: (row0 + s, Z_XBC // 256 + j)),
                  pl.BlockSpec((5, 256), lambda s, j: (0, j)),
                  pl.BlockSpec((1, 256), lambda s, j: (0, j))],
        out_specs=pl.BlockSpec((T, 256), lambda s, j: (s, j)),
        out_shape=jax.ShapeDtypeStruct((nseq * T, 1536), BF),
        scratch_shapes=[pltpu.VMEM((T + 16, 256), F32)],
        compiler_params=_cp("arbitrary", "arbitrary"),
        name=f"ssd_conv_T{T}",
    )(z, w, b)


SSD_CHUNK = 128


def _ssd_kernel(bias_ref, alog_ref, dpar_ref, x_ref, b_ref, c_ref, z_ref, sm_ref, h0_ref,
                y_ref, hf_ref, yacc, hst, *, T):
    h = pl.program_id(1)
    nc = T // SSD_CHUNK
    r = lax.broadcasted_iota(jnp.int32, (128, 128), 0)
    c = lax.broadcasted_iota(jnp.int32, (128, 128), 1)
    tril = jnp.where(r >= c, 1.0, 0.0).astype(F32)
    triu = jnp.where(r <= c, 1.0, 0.0).astype(F32)
    yacc[...] = jnp.zeros((T, 128), F32)
    hst[0] = h0_ref[0, 0, 0]
    hst[1] = h0_ref[0, 1, 0]

    def body(ci, carry):
        for d in range(2):
            blk = ci if d == 0 else nc - 1 - ci
            rows = pl.ds(pl.multiple_of(blk * SSD_CHUNK, SSD_CHUNK), SSD_CHUNK)
            sel = jnp.where(r == SM_DT + d * 8 + h, 1.0, 0.0).astype(BF)
            raw = jnp.dot(sm_ref[rows, :], sel, preferred_element_type=F32)
            dtb = _softplus(raw + bias_ref[d, h])
            a_neg = -jnp.exp(jnp.full((1, 128), alog_ref[d, h], F32))
            lab = dtb * a_neg
            cb = jnp.dot(tril if d == 0 else triu, lab, precision=HIGHEST, preferred_element_type=F32)
            c_t = cb.T
            dt_t = dtb.T
            msk = (r >= c) if d == 0 else (r <= c)
            seg = jnp.exp(jnp.where(msk, cb - c_t, -jnp.inf))
            cm = c_ref[rows, :]
            bm = b_ref[rows, :]
            xb = x_ref[rows, :]
            scores = lax.dot_general(cm, bm, NT, preferred_element_type=F32) * seg * dt_t
            hs = hst[d]
            y = jnp.dot(scores.astype(BF), xb, preferred_element_type=F32)
            y = y + lax.dot_general(cm, hs.astype(BF), NT, preferred_element_type=F32) * jnp.exp(cb)
            edge = cb[SSD_CHUNK - 1:SSD_CHUNK, :] if d == 0 else cb[0:1, :]
            wgt = jnp.exp(edge - cb) * dtb
            bw = (bm.astype(F32) * wgt).astype(BF)
            hst[d] = jnp.exp(edge) * hs + lax.dot_general(xb, bw, TN, preferred_element_type=F32)
            yacc[rows, :] += y
        return carry

    lax.fori_loop(0, nc, body, 0)
    hf_ref[0, 0, 0] = hst[0]
    hf_ref[0, 1, 0] = hst[1]

    def epi(ri, carry):
        rows = pl.ds(pl.multiple_of(ri * 256, 256), 256)
        y = yacc[rows, :] + dpar_ref[0, h] * x_ref[rows, :].astype(F32)
        y_ref[rows, :] = (y * _silu(z_ref[rows, :].astype(F32))).astype(BF)
        return carry

    lax.fori_loop(0, T // 256, epi, 0)


def _ssd(xbc, z, dt_bias, a_log, dpar, h0, *, nseq, T, row0):
    smem = pl.BlockSpec(memory_space=pltpu.SMEM)
    return pl.pallas_call(
        functools.partial(_ssd_kernel, T=T),
        grid=(nseq, 8),
        in_specs=[smem, smem, smem,
                  pl.BlockSpec((T, 128), lambda s, h: (s, h)),
                  pl.BlockSpec((T, 128), lambda s, h: (s, 8 + h // 4)),
                  pl.BlockSpec((T, 128), lambda s, h: (s, 10 + h // 4)),
                  pl.BlockSpec((T, 128), lambda s, h: (row0 + s, Z_BZ // 128 + h)),
                  pl.BlockSpec((T, 128), lambda s, h: (row0 + s, Z_SM // 128)),
                  pl.BlockSpec((1, 2, 1, 128, 128), lambda s, h: (s, 0, h, 0, 0))],
        out_specs=[pl.BlockSpec((T, 128), lambda s, h: (s, h)),
                   pl.BlockSpec((1, 2, 1, 128, 128), lambda s, h: (s, 0, h, 0, 0))],
        out_shape=[jax.ShapeDtypeStruct((nseq * T, 1024), BF),
                   jax.ShapeDtypeStruct((nseq, 2, 8, 128, 128), F32)],
        scratch_shapes=[pltpu.VMEM((T, 128), F32), pltpu.VMEM((2, 128, 128), F32)],
        compiler_params=_cp("arbitrary", "arbitrary"),
        name=f"ssd_scan_T{T}",
    )(dt_bias, a_log, dpar, xbc, xbc, xbc, z, z, h0)


def _rms_rows_kernel(x_ref, g_ref, o_ref):
    x = x_ref[...].astype(F32)
    o_ref[...] = (x * lax.rsqrt(jnp.mean(x * x, axis=-1, keepdims=True) + EPS) * g_ref[...]).astype(BF)


def _rms_rows(x, g):
    tm = 512
    n, w = x.shape
    return pl.pallas_call(
        _rms_rows_kernel,
        grid=(n // tm,),
        in_specs=[pl.BlockSpec((tm, w), lambda i: (i, 0)), pl.BlockSpec((1, w), lambda i: (0, 0))],
        out_specs=pl.BlockSpec((tm, w), lambda i: (i, 0)),
        out_shape=jax.ShapeDtypeStruct((n, w), BF),
        compiler_params=_cp("arbitrary"),
        name="ssd_out_norm",
    )(x, g.reshape(1, w))


def _rope_tables():
    t = np.arange(T_LAT)
    row = (t // GRID_W).astype(np.float32)
    col = (t % GRID_W).astype(np.float32)
    freqs = (ROPE_THETA ** (-np.arange(16, dtype=np.float32) / 16)).astype(np.float32)
    ang_r = row[:, None] * freqs[None, :]
    ang_c = col[:, None] * freqs[None, :]
    ang = np.concatenate([ang_r, ang_r, ang_c, ang_c], axis=-1).astype(np.float32)
    ang = np.concatenate([ang, ang], axis=-1)
    return jnp.cos(jnp.asarray(ang)), jnp.sin(jnp.asarray(ang))


def _rope(x, cos, sin):
    lane = lax.broadcasted_iota(jnp.int32, x.shape, 1)
    even_quarter = ((lane // 16) % 2) == 0
    partner = jnp.where(even_quarter, -pltpu.roll(x, 112, 1), pltpu.roll(x, 16, 1))
    return x * cos + partner * sin


def _mla_kv_kernel(src_ref, sm_ref, cos_ref, sin_ref, wkvb_ref, gkva_ref, gkn_ref, gkp_ref,
                   kh_ref, vh_ref, ckv_ref, *, norm_kv, rope):
    cc = src_ref[...].astype(F32)
    if norm_kv:
        cc = cc * lax.rsqrt(jnp.mean(cc * cc, axis=-1, keepdims=True) + EPS) * gkva_ref[...]
    ckv_ref[...] = cc
    kv = jnp.dot(cc.astype(BF), wkvb_ref[...], preferred_element_type=F32)
    sm = sm_ref[...].astype(F32)
    lane = lax.broadcasted_iota(jnp.int32, sm.shape, 1)
    pe = jnp.where(lane < 64, pltpu.roll(sm, 64, 1), 0.0)
    pe2 = jnp.sum(pe * pe, axis=-1, keepdims=True)
    for h in range(8):
        kn = kv[:, h * 256:h * 256 + 128]
        ri = lax.rsqrt((jnp.sum(kn * kn, axis=-1, keepdims=True) + pe2) * (1.0 / 192.0) + EPS)
        kh_ref[:, h * 256:h * 256 + 128] = (kn * ri * gkn_ref[...]).astype(BF)
        p = pe * ri * gkp_ref[...]
        if rope:
            p = _rope(p, cos_ref[...], sin_ref[...])
        kh_ref[:, h * 256 + 128:(h + 1) * 256] = p.astype(BF)
        vh_ref[:, h * 128:(h + 1) * 128] = kv[:, h * 256 + 128:(h + 1) * 256].astype(BF)


def _mla_kv(src, src_blk, sm, sm_blk, cos, sin, wkvb, gkva, gkn, gkp, *, rows, norm_kv, rope, name):
    tm = 256
    ntab = T_LAT // tm
    return pl.pallas_call(
        functools.partial(_mla_kv_kernel, norm_kv=norm_kv, rope=rope),
        grid=(rows // tm,),
        in_specs=[pl.BlockSpec((tm, 256), lambda i: (src_blk[0] + i, src_blk[1])),
                  pl.BlockSpec((tm, 128), lambda i: (sm_blk[0] + i, sm_blk[1])),
                  pl.BlockSpec((tm, 128), lambda i: (i % ntab, 0)),
                  pl.BlockSpec((tm, 128), lambda i: (i % ntab, 0)),
                  pl.BlockSpec((256, 2048), lambda i: (0, 0)),
                  pl.BlockSpec((1, 256), lambda i: (0, 0)),
                  pl.BlockSpec((1, 128), lambda i: (0, 0)),
                  pl.BlockSpec((1, 128), lambda i: (0, 0))],
        out_specs=[pl.BlockSpec((tm, 2048), lambda i: (i, 0)),
                   pl.BlockSpec((tm, 1024), lambda i: (i, 0)),
                   pl.BlockSpec((tm, 256), lambda i: (i, 0))],
        out_shape=[jax.ShapeDtypeStruct((rows, 2048), BF),
                   jax.ShapeDtypeStruct((rows, 1024), BF),
                   jax.ShapeDtypeStruct((rows, 256), F32)],
        compiler_params=_cp("arbitrary"),
        name=name,
    )(src, sm, cos, sin, wkvb, gkva, gkn, gkp)


def _mla_q_kernel(qa_ref, cos_ref, sin_ref, wqb_ref, gqa_ref, gq_ref, qh_ref, *, rope):
    qa = qa_ref[...].astype(F32)
    qa = qa * lax.rsqrt(jnp.mean(qa * qa, axis=-1, keepdims=True) + EPS) * gqa_ref[...]
    q = jnp.dot(qa.astype(BF), wqb_ref[...], preferred_element_type=F32)
    scale = 192.0 ** -0.5
    for h in range(8):
        qn = q[:, h * 256:h * 256 + 128]
        qp = q[:, h * 256 + 128:(h + 1) * 256]
        ss = jnp.sum(qn * qn, axis=-1, keepdims=True) + jnp.sum(qp * qp, axis=-1, keepdims=True)
        ri = lax.rsqrt(ss * (1.0 / 192.0) + EPS)
        qh_ref[:, h * 256:h * 256 + 128] = (qn * ri * gq_ref[:, 0:128] * scale).astype(BF)
        p = qp * ri * gq_ref[:, 128:256]
        if rope:
            p = _rope(p, cos_ref[...], sin_ref[...])
        qh_ref[:, h * 256 + 128:(h + 1) * 256] = (p * scale).astype(BF)


def _mla_q(z, row_blk0, cos, sin, wqb, gqa, gq, *, rows, rope, name):
    tm = 256
    ntab = T_LAT // tm
    return pl.pallas_call(
        functools.partial(_mla_q_kernel, rope=rope),
        grid=(rows // tm,),
        in_specs=[pl.BlockSpec((tm, 768), lambda i: (row_blk0 + i, Z_QA // 768)),
                  pl.BlockSpec((tm, 128), lambda i: (i % ntab, 0)),
                  pl.BlockSpec((tm, 128), lambda i: (i % ntab, 0)),
                  pl.BlockSpec((768, 2048), lambda i: (0, 0)),
                  pl.BlockSpec((1, 768), lambda i: (0, 0)),
                  pl.BlockSpec((1, 256), lambda i: (0, 0))],
        out_specs=pl.BlockSpec((tm, 2048), lambda i: (i, 0)),
        out_shape=jax.ShapeDtypeStruct((rows, 2048), BF),
        compiler_params=_cp("arbitrary"),
        name=name,
    )(z, cos, sin, wqb, gqa, gq)


def _mla_attn_kernel(q_ref, k_ref, v_ref, o_ref):
    s = lax.dot_general(q_ref[...], k_ref[...], NT, preferred_element_type=F32)
    p = jnp.exp(s - jnp.max(s, axis=-1, keepdims=True))
    l = jnp.sum(p, axis=-1, keepdims=True)
    o = jnp.dot(p.astype(BF), v_ref[...], preferred_element_type=F32)
    o_ref[...] = (o / l).astype(BF)


def _mla_attn(q, k, v, *, nb, tq_total, tk, name):
    tq = 256
    nq = tq_total // tq
    return pl.pallas_call(
        _mla_attn_kernel,
        grid=(nb, 8, nq),
        in_specs=[pl.BlockSpec((tq, 256), lambda b, h, i: (b * nq + i, h)),
                  pl.BlockSpec((tk, 256), lambda b, h, i: (b, h)),
                  pl.BlockSpec((tk, 128), lambda b, h, i: (b, h))],
        out_specs=pl.BlockSpec((tq, 128), lambda b, h, i: (b * nq + i, h)),
        out_shape=jax.ShapeDtypeStruct((nb * tq_total, 1024), BF),
        compiler_params=_cp("arbitrary", "arbitrary", "arbitrary"),
        name=name,
    )(q, k, v)


def _group64_norm(x, g):
    r = lax.broadcasted_iota(jnp.int32, (128, 128), 0)
    c = lax.broadcasted_iota(jnp.int32, (128, 128), 1)
    gm = jnp.where((r // 64) == (c // 64), 1.0, 0.0).astype(BF)
    outs = []
    for j in range(8):
        xj = x[:, j * 128:(j + 1) * 128]
        sq = xj * xj
        hi = sq.astype(BF)
        lo = (sq - hi.astype(F32)).astype(BF)
        ms = (jnp.dot(hi, gm, preferred_element_type=F32) + jnp.dot(lo, gm, preferred_element_type=F32)) * (1.0 / 64.0)
        outs.append(xj * lax.rsqrt(ms + EPS) * g)
    return outs


def _diff_prep_kernel(q_ref, k_ref, cos_ref, sin_ref, gq_ref, gk_ref, qh_ref, kh_ref, kc_ref, *, rope):
    qs = _group64_norm(q_ref[...].astype(F32), gq_ref[...])
    ks = _group64_norm(k_ref[...].astype(F32), gk_ref[...])
    scale = 64.0 ** -0.5
    for j in range(8):
        cols = slice(j * 128, (j + 1) * 128)
        qj, kj = qs[j], ks[j]
        kc_ref[:, cols] = kj
        if rope:
            qj = _rope(qj, cos_ref[...], sin_ref[...])
            kj = _rope(kj, cos_ref[...], sin_ref[...])
        qh_ref[:, cols] = (qj * scale).astype(BF)
        kh_ref[:, cols] = kj.astype(BF)


def _diff_prep(z, row_blk0, cos, sin, gq, gk, *, rows, rope, name):
    tm = 256
    ntab = T_LAT // tm
    return pl.pallas_call(
        functools.partial(_diff_prep_kernel, rope=rope),
        grid=(rows // tm,),
        in_specs=[pl.BlockSpec((tm, 1024), lambda i: (row_blk0 + i, Z_DQ // 1024)),
                  pl.BlockSpec((tm, 1024), lambda i: (row_blk0 + i, Z_DK // 1024)),
                  pl.BlockSpec((tm, 128), lambda i: (i % ntab, 0)),
                  pl.BlockSpec((tm, 128), lambda i: (i % ntab, 0)),
                  pl.BlockSpec((1, 128), lambda i: (0, 0)),
                  pl.BlockSpec((1, 128), lambda i: (0, 0))],
        out_specs=[pl.BlockSpec((tm, 1024), lambda i: (i, 0)),
                   pl.BlockSpec((tm, 1024), lambda i: (i, 0)),
                   pl.BlockSpec((tm, 1024), lambda i: (i, 0))],
        out_shape=[jax.ShapeDtypeStruct((rows, 1024), BF),
                   jax.ShapeDtypeStruct((rows, 1024), BF),
                   jax.ShapeDtypeStruct((rows, 1024), F32)],
        compiler_params=_cp("arbitrary"),
        name=name,
    )(z, z, cos, sin, gq, gk)


def _diff_attn_kernel(q_ref, k_ref, v_ref, lam_ref, g_ref, o_ref, *, lam_init):
    q = q_ref[...]
    k = k_ref[...]
    v = v_ref[...]
    lane = lax.broadcasted_iota(jnp.int32, q.shape, 1)
    zero = jnp.zeros_like(q)
    outs = []
    for m in range(2):
        qm = jnp.where((lane < 64) if m == 0 else (lane >= 64), q, zero)
        s = lax.dot_general(qm, k, NT, preferred_element_type=F32)
        p = jnp.exp(s - jnp.max(s, axis=-1, keepdims=True))
        l = jnp.sum(p, axis=-1, keepdims=True)
        outs.append(jnp.dot(p.astype(BF), v, preferred_element_type=F32) / l)
    lam = lam_ref[...]
    lam_full = (jnp.exp(jnp.sum(lam[0:1, :] * lam[1:2, :], axis=-1, keepdims=True))
                - jnp.exp(jnp.sum(lam[2:3, :] * lam[3:4, :], axis=-1, keepdims=True)) + lam_init)
    od = outs[0] - lam_full * outs[1]
    y = od * lax.rsqrt(jnp.mean(od * od, axis=-1, keepdims=True) + EPS) * g_ref[...]
    o_ref[...] = (y * (1.0 - lam_init)).astype(BF)


def _diff_attn(q, k, v, lam, g, *, nb, tq_total, tk, lam_init, name):
    tq = 256
    nq = tq_total // tq
    return pl.pallas_call(
        functools.partial(_diff_attn_kernel, lam_init=lam_init),
        grid=(nb, 8, nq),
        in_specs=[pl.BlockSpec((tq, 128), lambda b, h, i: (b * nq + i, h)),
                  pl.BlockSpec((tk, 128), lambda b, h, i: (b, h)),
                  pl.BlockSpec((tk, 128), lambda b, h, i: (b, h)),
                  pl.BlockSpec((4, 64), lambda b, h, i: (0, 0)),
                  pl.BlockSpec((1, 128), lambda b, h, i: (0, 0))],
        out_specs=pl.BlockSpec((tq, 128), lambda b, h, i: (b * nq + i, h)),
        out_shape=jax.ShapeDtypeStruct((nb * tq_total, 1024), BF),
        compiler_params=_cp("arbitrary", "arbitrary", "arbitrary"),
        name=name,
    )(q, k, v, lam, g)


PEER_TOPK = 16
_CAND_GROUPS = [(0, 16), (1, 8), (2, 5), (3, 4), (4, 3), (5, 2), (6, 2), (7, 2)]


def _top16_rows(s, payload=None):
    nrow = s.shape[0]
    row = lax.broadcasted_iota(jnp.int32, s.shape, 0)
    vals, idxs, pays = [], [], [[] for _ in (payload or [])]
    for _ in range(PEER_TOPK):
        m = jnp.max(s, axis=0, keepdims=True)
        i = jnp.min(jnp.where(s == m, row, nrow), axis=0, keepdims=True)
        hit = row == i
        vals.append(m)
        idxs.append(i)
        for a, pay in enumerate(payload or []):
            pays[a].append(jnp.max(jnp.where(hit, pay, -1), axis=0, keepdims=True))
        s = jnp.where(hit, -jnp.inf, s)
    return (jnp.concatenate(vals, axis=0), jnp.concatenate(idxs, axis=0),
            [jnp.concatenate(p, axis=0) for p in pays])


def _peer_route_kernel(q_ref, keys_ref, w_ref, a_s, b_s, g_s, *, tn):
    sub = lax.broadcasted_iota(jnp.int32, (8, tn), 0)
    for h in range(8):
        tops = []
        for p in range(2):
            qhp = q_ref[:, (h * 2 + p) * 128:(h * 2 + p + 1) * 128]
            st = lax.dot_general(keys_ref[h, p], qhp, NT, preferred_element_type=F32)
            v, i, _ = _top16_rows(st)
            tops.append((v, i))
        (s1, i1), (s2, i2) = tops
        cs, ca, cb = [], [], []
        for r, nvalid in _CAND_GROUPS:
            width = 16 if r == 0 else 8
            blk = s1[r:r + 1, :] + s2[0:width, :]
            if nvalid < width:
                blk = jnp.where(sub < nvalid, blk, -jnp.inf)
            cs.append(blk)
            ca.append(jnp.broadcast_to(i1[r:r + 1, :], (width, tn)))
            cb.append(i2[0:width, :])
        cs.append(s1[8:16, :] + s2[0:1, :])
        ca.append(i1[8:16, :])
        cb.append(jnp.broadcast_to(i2[0:1, :], (8, tn)))
        cand_s = jnp.concatenate(cs, axis=0)
        cand_a = jnp.concatenate(ca, axis=0)
        cand_b = jnp.concatenate(cb, axis=0)
        top_s, _, (sel_a, sel_b) = _top16_rows(cand_s, [cand_a, cand_b])
        e = jnp.exp(top_s - top_s[0:1, :])
        g = e / jnp.sum(e, axis=0, keepdims=True)
        a_s[:, h * 16:(h + 1) * 16] = sel_a.astype(F32).T
        b_s[:, h * 16:(h + 1) * 16] = sel_b.astype(F32).T
        g_s[:, h * 16:(h + 1) * 16] = g.T

    ids = lax.broadcasted_iota(jnp.int32, (128, 128), 0).astype(F32)

    def build(n, carry):
        a_row = a_s[pl.ds(n, 1), :]
        b_row = b_s[pl.ds(n, 1), :]
        g_row = g_s[pl.ds(n, 1), :]
        pt = jnp.where(ids == a_row, g_row, 0.0).astype(BF)
        qt = jnp.where(ids == b_row, 1.0, 0.0).astype(BF)
        w_ref[n] = lax.dot_general(pt, qt, NT, preferred_element_type=F32).astype(BF)
        return carry

    lax.fori_loop(0, tn, build, 0, unroll=4)


def _peer_route(q, keys):
    tn = 128
    n = q.shape[0]
    return pl.pallas_call(
        functools.partial(_peer_route_kernel, tn=tn),
        grid=(n // tn,),
        in_specs=[pl.BlockSpec((tn, 2048), lambda i: (i, 0)),
                  pl.BlockSpec((8, 2, 128, 128), lambda i: (0, 0, 0, 0))],
        out_specs=pl.BlockSpec((tn, 128, 128), lambda i: (i, 0, 0)),
        out_shape=jax.ShapeDtypeStruct((n, 128, 128), BF),
        scratch_shapes=[pltpu.VMEM((tn, 128), F32), pltpu.VMEM((tn, 128), F32), pltpu.VMEM((tn, 128), F32)],
        compiler_params=_cp("arbitrary"),
        name="peer_route",
    )(q, keys)


def _gelu_tanh(x):
    return 0.5 * x * (1.0 + jnp.tanh(0.7978845608028654 * (x + 0.044715 * x * x * x)))


def _peer_dense_kernel(x_ref, u_ref, v_ref, w_ref, o_ref):
    @pl.when(pl.program_id(1) == 0)
    def _():
        o_ref[...] = jnp.zeros_like(o_ref)

    hid = lax.dot_general(x_ref[...], u_ref[...], NT, preferred_element_type=F32)
    act = (_gelu_tanh(hid) * w_ref[...].astype(F32)).astype(BF)
    o_ref[...] += jnp.dot(act, v_ref[...], preferred_element_type=F32)


def _peer_dense(x, u, v, w):
    tn, te = 512, 256
    n = x.shape[0]
    ne = u.shape[0]
    return pl.pallas_call(
        _peer_dense_kernel,
        grid=(n // tn, ne // te),
        in_specs=[pl.BlockSpec((tn, D), lambda i, e: (i, 0)),
                  pl.BlockSpec((te, D), lambda i, e: (e, 0)),
                  pl.BlockSpec((te, D), lambda i, e: (e, 0)),
                  pl.BlockSpec((tn, te), lambda i, e: (i, e))],
        out_specs=pl.BlockSpec((tn, D), lambda i, e: (i, 0)),
        out_shape=jax.ShapeDtypeStruct((n, D), F32),
        compiler_params=_cp("arbitrary", "arbitrary"),
        name="peer_dense",
    )(x, u, v, w)


def _permute_w_in(w):
    o = dict(a_q=0, a_k=512, a_v=1024, a_g=2048, a_gk=3072, b_z=3104, b_xbc=4128, b_dt=5664,
             c_qa=5680, c_kva=6448, c_kpe=6704, d_q=6768, d_k=7792, d_v=8816)

    def seg(name, width):
        return w[:, o[name]:o[name] + width]

    zeros = lambda n: jnp.zeros((w.shape[0], n), w.dtype)
    parts = [seg('a_v', 1024), seg('a_g', 1024), seg('b_z', 1024), seg('d_q', 1024), seg('d_k', 1024),
             seg('d_v', 1024), seg('b_xbc', 1536), seg('a_q', 512), seg('a_k', 512), seg('c_kva', 256),
             seg('a_gk', 32), seg('b_dt', 16), zeros(16), seg('c_kpe', 64), zeros(128), seg('c_qa', 768)]
    return jnp.concatenate(parts, axis=1).astype(BF)


def _permute_w_qb(w):
    w3 = w.reshape(w.shape[0], 8, 192)
    return jnp.concatenate([w3, jnp.zeros((w.shape[0], 8, 64), w.dtype)], axis=-1).reshape(w.shape[0], 2048).astype(BF)


def kernel(x_prompt, x_sample, cache_mla_ckv, cache_mla_kpe, cache_diff_k, cache_diff_v, state_gla, state_ssd, c, c_ctx, norm1_g, norm2_g, w_ada, b_ada, w_in, w_out, gla_w_gk, gla_b_gk, gla_norm_g, ssd_conv_w, ssd_conv_b, ssd_a_log, ssd_dt_bias, ssd_d, ssd_norm_g, mla_qa_norm_g, mla_w_qb, mla_kva_norm_g, mla_w_kvb, mla_q_norm_g, mla_k_norm_g, diff_q_norm_g, diff_k_norm_g, diff_lambda, diff_subln_g, peer_w_q, peer_sub_keys, peer_u, peer_v):
    depth = w_in.shape[0]
    x = jnp.concatenate([x_prompt.reshape(N_CTX, D), x_sample.reshape(N_LAT, D)], axis=0)
    cv8 = jnp.concatenate([c_ctx[None, :], c, jnp.zeros((5, D), F32)], axis=0)
    cos_t, sin_t = _rope_tables()
    nct = N_CTX // T_CTX
    outs = {k: [] for k in ('ckv', 'kpe', 'dk', 'dv', 'gla', 'ssd')}

    for l in range(depth):
        mod3 = _modulation(cv8, w_ada, b_ada, l).reshape(8, 6, D)[:3]

        (z,) = _norm_mm(x, norm1_g[l], mod3, _permute_w_in(w_in[l]), shift_row=0, tn=768, emit_xn=False,
                        name="norm_in_proj")

        wgk = gla_w_gk[l]
        wgk_pad = jnp.zeros((2, 128, 512), F32)
        wgk_pad = wgk_pad.at[0, SM_GK:SM_GK + 16].set(wgk[0]).at[1, SM_GK + 16:SM_GK + 32].set(wgk[1]).astype(BF)
        bgk = gla_b_gk[l].reshape(2, 1, 512)
        ng = gla_norm_g[l].reshape(1, 128)
        a_ctx, gla_fin = _gla(z, wgk_pad, bgk, ng, jnp.zeros((B_CTX, 2, 8, 64, 128), F32),
                              nseq=B_CTX, T=T_CTX, row0=0)
        a_lat, _ = _gla(z, wgk_pad, bgk, ng, state_gla[:, l], nseq=B_LAT, T=T_LAT, row0=N_CTX // T_LAT)
        out_a = jnp.concatenate([a_ctx, a_lat], axis=0)

        conv_b = ssd_conv_b[l].reshape(1, 1536)
        xbc_ctx = _conv(z, ssd_conv_w[l], conv_b, nseq=B_CTX, T=T_CTX, row0=0)
        xbc_lat = _conv(z, ssd_conv_w[l], conv_b, nseq=B_LAT, T=T_LAT, row0=N_CTX // T_LAT)
        dpar = ssd_d[l].reshape(1, 8)
        y_ctx, ssd_fin = _ssd(xbc_ctx, z, ssd_dt_bias[l], ssd_a_log[l], dpar,
                              jnp.zeros((B_CTX, 2, 8, 128, 128), F32), nseq=B_CTX, T=T_CTX, row0=0)
        y_lat, _ = _ssd(xbc_lat, z, ssd_dt_bias[l], ssd_a_log[l], dpar, state_ssd[:, l],
                        nseq=B_LAT, T=T_LAT, row0=N_CTX // T_LAT)
        out_b = _rms_rows(jnp.concatenate([y_ctx, y_lat], axis=0), ssd_norm_g[l])

        wkvb = mla_w_kvb[l].astype(BF)
        wqb = _permute_w_qb(mla_w_qb[l])
        gkva = mla_kva_norm_g[l].reshape(1, 256)
        gk = mla_k_norm_g[l]
        gkn = gk[:128].reshape(1, 128)
        gkp = jnp.concatenate([gk[128:], jnp.zeros((64,), F32)]).reshape(1, 128)
        gq = jnp.concatenate([mla_q_norm_g[l], jnp.zeros((64,), F32)]).reshape(1, 256)
        gqa = mla_qa_norm_g[l].reshape(1, 768)
        kh_ctx, vh_ctx, ckv_ctx = _mla_kv(z, (0, Z_KVA // 256), z, (0, Z_SM // 128), cos_t, sin_t, wkvb, gkva, gkn, gkp,
                                          rows=N_CTX, norm_kv=True, rope=False, name="mla_kv_ctx")
        kh_lat, vh_lat, _ = _mla_kv(z, (N_CTX // 256, Z_KVA // 256), z, (N_CTX // 256, Z_SM // 128), cos_t, sin_t,
                                    wkvb, gkva, gkn, gkp, rows=N_LAT, norm_kv=True, rope=True, name="mla_kv_lat")
        cache_sm = jnp.concatenate([jnp.zeros((B_LAT * PAST, 64), F32),
                                    cache_mla_kpe[:, l].reshape(B_LAT * PAST, 64)], axis=1)
        kh_past, vh_past, _ = _mla_kv(cache_mla_ckv[:, l].reshape(B_LAT * PAST, 256), (0, 0), cache_sm, (0, 0),
                                      cos_t, sin_t, wkvb, gkva, gkn, gkp,
                                      rows=B_LAT * PAST, norm_kv=False, rope=False, name="mla_kv_past")
        qh_ctx = _mla_q(z, 0, cos_t, sin_t, wqb, gqa, gq, rows=N_CTX, rope=False, name="mla_q_ctx")
        qh_lat = _mla_q(z, N_CTX // 256, cos_t, sin_t, wqb, gqa, gq, rows=N_LAT, rope=True, name="mla_q_lat")
        tk_lat = PAST + T_LAT
        k_lat = jnp.concatenate([kh_past.reshape(B_LAT, PAST, 2048), kh_lat.reshape(B_LAT, T_LAT, 2048)],
                                axis=1).reshape(B_LAT * tk_lat, 2048)
        v_lat = jnp.concatenate([vh_past.reshape(B_LAT, PAST, 1024), vh_lat.reshape(B_LAT, T_LAT, 1024)],
                                axis=1).reshape(B_LAT * tk_lat, 1024)
        c_ctx_o = _mla_attn(qh_ctx, kh_ctx, vh_ctx, nb=B_CTX, tq_total=T_CTX, tk=T_CTX, name="mla_attn_ctx")
        c_lat_o = _mla_attn(qh_lat, k_lat, v_lat, nb=B_LAT, tq_total=T_LAT, tk=tk_lat, name="mla_attn_lat")
        out_c = jnp.concatenate([c_ctx_o, c_lat_o], axis=0)

        gdq = jnp.tile(diff_q_norm_g[l], 2).reshape(1, 128)
        gdk = jnp.tile(diff_k_norm_g[l], 2).reshape(1, 128)
        dq_ctx, dk_ctx, dkc_ctx = _diff_prep(z, 0, cos_t, sin_t, gdq, gdk, rows=N_CTX, rope=False, name="diff_prep_ctx")
        dq_lat, dk_lat, _ = _diff_prep(z, N_CTX // 256, cos_t, sin_t, gdq, gdk, rows=N_LAT, rope=True,
                                       name="diff_prep_lat")
        dv_all = z[:, Z_DV:Z_DV + 1024]
        dk_full = jnp.concatenate([cache_diff_k[:, l].reshape(B_LAT, PAST, 1024).astype(BF),
                                   dk_lat.reshape(B_LAT, T_LAT, 1024)], axis=1).reshape(B_LAT * tk_lat, 1024)
        dv_full = jnp.concatenate([cache_diff_v[:, l].reshape(B_LAT, PAST, 1024).astype(BF),
                                   dv_all[N_CTX:].reshape(B_LAT, T_LAT, 1024)], axis=1).reshape(B_LAT * tk_lat, 1024)
        lam_init = 0.8 - 0.6 * math.exp(-0.3 * l)
        gsub = diff_subln_g[l].reshape(1, 128)
        d_ctx_o = _diff_attn(dq_ctx, dk_ctx, dv_all[:N_CTX], diff_lambda[l], gsub, nb=B_CTX, tq_total=T_CTX, tk=T_CTX,
                             lam_init=lam_init, name="diff_attn_ctx")
        d_lat_o = _diff_attn(dq_lat, dk_full, dv_full, diff_lambda[l], gsub, nb=B_LAT, tq_total=T_LAT, tk=tk_lat,
                             lam_init=lam_init, name="diff_attn_lat")
        out_d = jnp.concatenate([d_ctx_o, d_lat_o], axis=0)

        hmid = _out_proj((out_a, out_b, out_c, out_d), w_out[l].astype(BF), x, mod3)

        pq, u2 = _norm_mm(hmid, norm2_g[l], mod3, peer_w_q[l].astype(BF), shift_row=3, tn=512, emit_xn=True,
                          name="norm_peer_query")
        wmap = _peer_route(pq, peer_sub_keys[l].astype(BF)).reshape(N_TOK, 128 * 128)
        pe_out = _peer_dense(u2, peer_u[l].astype(BF), peer_v[l].astype(BF), wmap)
        x = _resid(hmid, pe_out, mod3)

        outs['ckv'].append(ckv_ctx.reshape(B_CTX, T_CTX, 256))
        outs['kpe'].append(z[:N_CTX, Z_SM + SM_KPE:Z_SM + 128].astype(F32).reshape(B_CTX, T_CTX, 64))
        outs['dk'].append(dkc_ctx.reshape(B_CTX, T_CTX, 8, 2, 64))
        outs['dv'].append(dv_all[:N_CTX].astype(F32).reshape(B_CTX, T_CTX, 8, 128))
        outs['gla'].append(gla_fin)
        outs['ssd'].append(ssd_fin)

    y_p = x[:N_CTX].reshape(B_CTX, T_CTX, D)
    y_s = x[N_CTX:].reshape(B_LAT, T_LAT, D)
    return (y_p, y_s, jnp.stack(outs['ckv'], axis=1), jnp.stack(outs['kpe'], axis=1),
            jnp.stack(outs['dk'], axis=1), jnp.stack(outs['dv'], axis=1),
            jnp.stack(outs['gla'], axis=1), jnp.stack(outs['ssd'], axis=1))
```

```python
import functools
import math

import jax
import jax.numpy as jnp
import numpy as np
from jax import lax
from jax.experimental import pallas as pl
from jax.experimental.pallas import tpu as pltpu

F32 = jnp.float32
BF = jnp.bfloat16
HIGHEST = lax.Precision.HIGHEST
LOG2E = 1.4426950408889634

D = 4096
EPS = 1e-6
N_CTX = 4096
N_LAT = 8192
N_TOK = N_CTX + N_LAT
MOD_ROWS = 4096
T_CTX, B_CTX = 256, 16
T_LAT, B_LAT = 4096, 2
PAST = 512
GRID_W = 64
ROPE_THETA = 10000.0

V7X_VMEM_LIMIT = 56 * 1024 * 1024

Z_AV, Z_AG, Z_BZ, Z_DQ, Z_DK, Z_DV = 0, 1024, 2048, 3072, 4096, 5120
Z_XBC, Z_AQ, Z_AK, Z_KVA, Z_SM, Z_QA = 6144, 7680, 8192, 8704, 8960, 9216
Z_W = 9984
SM_GK, SM_DT, SM_KPE = 0, 32, 64

NT = (((1,), (1,)), ((), ()))
TN = (((0,), (0,)), ((), ()))


def _cp(*sem):
    return pltpu.CompilerParams(dimension_semantics=sem, vmem_limit_bytes=V7X_VMEM_LIMIT)


def _sigmoid(x):
    return 1.0 / (1.0 + jnp.exp(-x))


def _silu(x):
    return x * _sigmoid(x)


def _softplus(x):
    return jnp.maximum(x, 0.0) + jnp.log1p(jnp.exp(-jnp.abs(x)))


def _log_sigmoid(x):
    return jnp.minimum(x, 0.0) - jnp.log1p(jnp.exp(-jnp.abs(x)))


def _mod_kernel(c_ref, w_ref, b_ref, o_ref):
    a = _silu(c_ref[...])
    o_ref[...] = jnp.dot(a.astype(BF), w_ref[...].astype(BF), preferred_element_type=F32) + b_ref[...]


def _modulation(cv8, w_ada, b_ada, layer):
    tn = 512
    return pl.pallas_call(
        _mod_kernel,
        grid=(6 * D // tn,),
        in_specs=[pl.BlockSpec((8, D), lambda j: (0, 0)),
                  pl.BlockSpec((None, D, tn), lambda j: (layer, 0, j)),
                  pl.BlockSpec((None, 1, tn), lambda j: (layer, 0, j))],
        out_specs=pl.BlockSpec((8, tn), lambda j: (0, j)),
        out_shape=jax.ShapeDtypeStruct((8, 6 * D), F32),
        compiler_params=_cp("arbitrary"),
        name="modulation",
    )(cv8, w_ada, b_ada.reshape(b_ada.shape[0], 1, 6 * D))


def _norm_mm_kernel(x_ref, g_ref, mod_ref, w_ref, o_ref, *rest, shift_row, emit_xn):
    if emit_xn:
        xn_out_ref, xn = rest
    else:
        (xn,) = rest

    @pl.when(pl.program_id(1) == 0)
    def _():
        m = mod_ref[0]
        gain = g_ref[...] * (1.0 + m[shift_row + 1:shift_row + 2, :])
        shift = m[shift_row:shift_row + 1, :]

        def chunk(r, carry):
            rows = pl.ds(pl.multiple_of(r * 64, 64), 64)
            x = x_ref[rows, :]
            u = (x * lax.rsqrt(jnp.mean(x * x, axis=-1, keepdims=True) + EPS) * gain + shift).astype(BF)
            xn[rows, :] = u
            if emit_xn:
                xn_out_ref[rows, :] = u
            return carry

        lax.fori_loop(0, x_ref.shape[0] // 64, chunk, 0)

    o_ref[...] = jnp.dot(xn[...], w_ref[...], preferred_element_type=F32).astype(o_ref.dtype)


def _norm_mm(x, g, mod3, w, layer, *, shift_row, tn, emit_xn, name):
    tm = 512
    n, k = x.shape
    nout = w.shape[2]
    out_shape = [jax.ShapeDtypeStruct((n, nout), BF)]
    out_specs = [pl.BlockSpec((tm, tn), lambda i, j: (i, j))]
    if emit_xn:
        out_shape.append(jax.ShapeDtypeStruct((n, k), BF))
        out_specs.append(pl.BlockSpec((tm, k), lambda i, j: (i, 0)))
    res = pl.pallas_call(
        functools.partial(_norm_mm_kernel, shift_row=shift_row, emit_xn=emit_xn),
        grid=(n // tm, nout // tn),
        in_specs=[pl.BlockSpec((tm, k), lambda i, j: (i, 0)),
                  pl.BlockSpec((1, k), lambda i, j: (0, 0)),
                  pl.BlockSpec((1, 6, k), lambda i, j: (i * tm // MOD_ROWS, 0, 0)),
                  pl.BlockSpec((None, k, tn), lambda i, j: (layer, 0, j))],
        out_specs=out_specs,
        out_shape=out_shape,
        scratch_shapes=[pltpu.VMEM((tm, k), BF)],
        compiler_params=_cp("arbitrary", "arbitrary"),
        name=name,
    )(x, g.reshape(1, k), mod3, w)
    return res


def _out_proj_kernel(a_ref, b_ref, c_ref, d_ref, w_ref, x_ref, mod_ref, o_ref):
    acc = jnp.dot(a_ref[...], w_ref[0:1024, :], preferred_element_type=F32)
    acc += jnp.dot(b_ref[...], w_ref[1024:2048, :], preferred_element_type=F32)
    acc += jnp.dot(c_ref[...], w_ref[2048:3072, :], preferred_element_type=F32)
    acc += jnp.dot(d_ref[...], w_ref[3072:4096, :], preferred_element_type=F32)
    o_ref[...] = x_ref[...] + mod_ref[0][2:3, :] * acc


def _out_proj(mix, w, layer, x, mod3):
    tm, tn = 512, 1024
    n = x.shape[0]
    mspec = pl.BlockSpec((tm, 1024), lambda i, j: (i, 0))
    return pl.pallas_call(
        _out_proj_kernel,
        grid=(n // tm, D // tn),
        in_specs=[mspec, mspec, mspec, mspec,
                  pl.BlockSpec((None, D, tn), lambda i, j: (layer, 0, j)),
                  pl.BlockSpec((tm, tn), lambda i, j: (i, j)),
                  pl.BlockSpec((1, 6, tn), lambda i, j: (i * tm // MOD_ROWS, 0, j))],
        out_specs=pl.BlockSpec((tm, tn), lambda i, j: (i, j)),
        out_shape=jax.ShapeDtypeStruct((n, D), F32),
        compiler_params=_cp("arbitrary", "arbitrary"),
        name="out_proj",
    )(*mix, w, x, mod3)


def _resid_kernel(h_ref, p_ref, mod_ref, o_ref):
    o_ref[...] = h_ref[...] + mod_ref[0][5:6, :] * p_ref[...]


def _resid(h, p, mod3, row0=0, rows=None):
    tm = 256
    n = h.shape[0] if rows is None else rows
    blk0 = row0 // tm
    spec = pl.BlockSpec((tm, D), lambda i: (blk0 + i, 0))
    return pl.pallas_call(
        _resid_kernel,
        grid=(n // tm,),
        in_specs=[spec, spec, pl.BlockSpec((1, 6, D), lambda i: ((blk0 + i) * tm // MOD_ROWS, 0, 0))],
        out_specs=pl.BlockSpec((tm, D), lambda i: (i, 0)),
        out_shape=jax.ShapeDtypeStruct((n, D), F32),
        compiler_params=_cp("arbitrary"),
        name="peer_residual",
    )(h, p, mod3)


GLA_BLK = 16


def _gla_kernel(q_ref, k_ref, v_ref, g_ref, sm_ref, wgk_ref, bgk_ref, ng_ref, s0_ref,
                o_ref, sf_ref, oacc, cumf, cumb, st, *, T):
    nb = T // GLA_BLK
    r256 = lax.broadcasted_iota(jnp.int32, (256, 256), 0)
    c256 = lax.broadcasted_iota(jnp.int32, (256, 256), 1)
    same = (r256 // GLA_BLK) == (c256 // GLA_BLK)
    lower = jnp.where(same, jnp.where(r256 >= c256, 1.0, 0.0), 0.0).astype(F32)
    upper = jnp.where(same, jnp.where(r256 <= c256, 1.0, 0.0), 0.0).astype(F32)

    def gate_body(r, carry):
        rows = pl.ds(pl.multiple_of(r * 256, 256), 256)
        smb = sm_ref[rows, :]
        gf = jnp.dot(smb, wgk_ref[0], preferred_element_type=F32) + bgk_ref[0]
        gb = jnp.dot(smb, wgk_ref[1], preferred_element_type=F32) + bgk_ref[1]
        laf = _log_sigmoid(gf) * (1.0 / 16.0)
        lab = _log_sigmoid(gb) * (1.0 / 16.0)
        cumf[rows, :] = jnp.dot(lower, laf, precision=HIGHEST, preferred_element_type=F32)
        cumb[rows, :] = jnp.dot(upper, lab, precision=HIGHEST, preferred_element_type=F32)
        oacc[rows, :] = jnp.zeros((256, 256), F32)
        return carry

    lax.fori_loop(0, T // 256, gate_body, 0)

    for d in range(2):
        for h in range(2):
            st[d, h] = s0_ref[0, d, h].T

    scale = 64.0 ** -0.5
    rowi = lax.broadcasted_iota(jnp.int32, (GLA_BLK, 128), 0)
    lane = lax.broadcasted_iota(jnp.int32, (GLA_BLK, 128), 1)
    lo = lane < 64

    def body(i, carry):
        for d in range(2):
            blk = i if d == 0 else nb - 1 - i
            rows = pl.ds(pl.multiple_of(blk * GLA_BLK, GLA_BLK), GLA_BLK)
            qb = q_ref[rows, :].astype(F32) * scale
            kb = k_ref[rows, :].astype(F32)
            vb = v_ref[rows, :]
            vf = vb.astype(F32)
            cb = cumf[rows, :] if d == 0 else cumb[rows, :]
            edge = cb[GLA_BLK - 1:GLA_BLK, :] if d == 0 else cb[0:1, :]
            qh = qb * jnp.exp(cb)
            kt = kb * jnp.exp(edge - cb)
            dec = jnp.exp(edge)
            od0 = jnp.zeros((GLA_BLK, 128), F32)
            od1 = jnp.zeros((GLA_BLK, 128), F32)
            for s in range(GLA_BLK):
                msk = (rowi >= s) if d == 0 else (rowi <= s)
                w = qb * kb[s:s + 1, :] * jnp.exp(jnp.minimum(cb - cb[s:s + 1, :], 0.0))
                w = jnp.where(msk, w, 0.0)
                a0 = jnp.sum(jnp.where(lo, w, 0.0), axis=1, keepdims=True)
                a1 = jnp.sum(jnp.where(lo, 0.0, w), axis=1, keepdims=True)
                od0 = od0 + a0 * vf[s:s + 1, 0:128]
                od1 = od1 + a1 * vf[s:s + 1, 128:256]
            for h in range(2):
                hs = slice(h * 64, h * 64 + 64)
                s_t = st[d, h]
                o_h = lax.dot_general(qh[:, hs].astype(BF), s_t.astype(BF), NT, preferred_element_type=F32)
                o_h = o_h + (od0 if h == 0 else od1)
                upd = lax.dot_general(vb[:, h * 128:(h + 1) * 128], kt[:, hs].astype(BF), TN,
                                      preferred_element_type=F32)
                st[d, h] = s_t * dec[:, hs] + upd
                oacc[rows, h * 128:(h + 1) * 128] += o_h
        return carry

    lax.fori_loop(0, nb, body, 0, unroll=2)

    for d in range(2):
        for h in range(2):
            sf_ref[0, d, h] = st[d, h].T

    def epi(r, carry):
        rows = pl.ds(pl.multiple_of(r * 256, 256), 256)
        for h in range(2):
            cols = slice(h * 128, (h + 1) * 128)
            o = oacc[rows, cols]
            y = o * lax.rsqrt(jnp.mean(o * o, axis=-1, keepdims=True) + EPS) * ng_ref[...]
            o_ref[rows, cols] = (y * _silu(g_ref[rows, cols].astype(F32))).astype(BF)
        return carry

    lax.fori_loop(0, T // 256, epi, 0)


def _gla(z, wgk_pad, bgk, ng, s0, *, nseq, T, row0):
    return pl.pallas_call(
        functools.partial(_gla_kernel, T=T),
        grid=(nseq, 4),
        in_specs=[pl.BlockSpec((T, 128), lambda s, p: (row0 + s, Z_AQ // 128 + p)),
                  pl.BlockSpec((T, 128), lambda s, p: (row0 + s, Z_AK // 128 + p)),
                  pl.BlockSpec((T, 256), lambda s, p: (row0 + s, Z_AV // 256 + p)),
                  pl.BlockSpec((T, 256), lambda s, p: (row0 + s, Z_AG // 256 + p)),
                  pl.BlockSpec((T, 128), lambda s, p: (row0 + s, Z_SM // 128)),
                  pl.BlockSpec((2, 128, 128), lambda s, p: (0, 0, p)),
                  pl.BlockSpec((2, 1, 128), lambda s, p: (0, 0, p)),
                  pl.BlockSpec((1, 128), lambda s, p: (0, 0)),
                  pl.BlockSpec((1, 2, 2, 64, 128), lambda s, p: (s, 0, p, 0, 0))],
        out_specs=[pl.BlockSpec((T, 256), lambda s, p: (s, p)),
                   pl.BlockSpec((1, 2, 2, 64, 128), lambda s, p: (s, 0, p, 0, 0))],
        out_shape=[jax.ShapeDtypeStruct((nseq * T, 1024), BF),
                   jax.ShapeDtypeStruct((nseq, 2, 8, 64, 128), F32)],
        scratch_shapes=[pltpu.VMEM((T, 256), F32), pltpu.VMEM((T, 128), F32), pltpu.VMEM((T, 128), F32),
                        pltpu.VMEM((2, 2, 128, 64), F32)],
        compiler_params=_cp("arbitrary", "arbitrary"),
        name=f"gla_T{T}",
    )(z, z, z, z, z, wgk_pad, bgk, ng, s0)


def _conv_kernel(x_ref, w_ref, b_ref, o_ref, xp, *, T):
    xp[0:8, :] = jnp.zeros((8, 256), F32)
    xp[8 + T:16 + T, :] = jnp.zeros((8, 256), F32)
    xp[8:8 + T, :] = x_ref[...].astype(F32)
    for r in range(T // 256):
        acc = b_ref[...] + w_ref[0:1, :] * xp[6 + r * 256:6 + (r + 1) * 256, :]
        for kk in range(1, 5):
            acc = acc + w_ref[kk:kk + 1, :] * xp[6 + kk + r * 256:6 + kk + (r + 1) * 256, :]
        o_ref[r * 256:(r + 1) * 256, :] = _silu(acc).astype(BF)


def _conv(z, w, b, *, nseq, T, row0):
    return pl.pallas_call(
        functools.partial(_conv_kernel, T=T),
        grid=(nseq, 6),
        in_specs=[pl.BlockSpec((T, 256), lambda s, j: (row0 + s, Z_XBC // 256 + j)),
                  pl.BlockSpec((5, 256), lambda s, j: (0, j)),
                  pl.BlockSpec((1, 256), lambda s, j: (0, j))],
        out_specs=pl.BlockSpec((T, 256), lambda s, j: (s, j)),
        out_shape=jax.ShapeDtypeStruct((nseq * T, 1536), BF),
        scratch_shapes=[pltpu.VMEM((T + 16, 256), F32)],
        compiler_params=_cp("arbitrary", "arbitrary"),
        name=f"ssd_conv_T{T}",
    )(z, w, b)


SSD_CHUNK = 128


def _ssd_kernel(bias_ref, alog_ref, dpar_ref, x_ref, b_ref, c_ref, z_ref, sm_ref, h0_ref,
                y_ref, hf_ref, yacc, hst, *, T):
    h = pl.program_id(1)
    nc = T // SSD_CHUNK
    r = lax.broadcasted_iota(jnp.int32, (128, 128), 0)
    c = lax.broadcasted_iota(jnp.int32, (128, 128), 1)
    tril = jnp.where(r >= c, 1.0, 0.0).astype(F32)
    triu = jnp.where(r <= c, 1.0, 0.0).astype(F32)
    yacc[...] = jnp.zeros((T, 128), F32)
    hst[0] = h0_ref[0, 0, 0]
    hst[1] = h0_ref[0, 1, 0]

    def body(ci, carry):
        for d in range(2):
            blk = ci if d == 0 else nc - 1 - ci
            rows = pl.ds(pl.multiple_of(blk * SSD_CHUNK, SSD_CHUNK), SSD_CHUNK)
            sel = jnp.where(r == SM_DT + d * 8 + h, 1.0, 0.0).astype(BF)
            raw = jnp.dot(sm_ref[rows, :], sel, preferred_element_type=F32)
            dtb = _softplus(raw + bias_ref[d, h])
            a_neg = -jnp.exp(jnp.full((1, 128), alog_ref[d, h], F32))
            lab = dtb * a_neg
            cb = jnp.dot(tril if d == 0 else triu, lab, precision=HIGHEST, preferred_element_type=F32)
            c_t = cb.T
            dt_t = dtb.T
            msk = (r >= c) if d == 0 else (r <= c)
            seg = jnp.exp(jnp.where(msk, cb - c_t, -jnp.inf))
            cm = c_ref[rows, :]
            bm = b_ref[rows, :]
            xb = x_ref[rows, :]
            scores = lax.dot_general(cm, bm, NT, preferred_element_type=F32) * seg * dt_t
            hs = hst[d]
            y = jnp.dot(scores.astype(BF), xb, preferred_element_type=F32)
            y = y + lax.dot_general(cm, hs.astype(BF), NT, preferred_element_type=F32) * jnp.exp(cb)
            edge = cb[SSD_CHUNK - 1:SSD_CHUNK, :] if d == 0 else cb[0:1, :]
            wgt = jnp.exp(edge - cb) * dtb
            bw = (bm.astype(F32) * wgt).astype(BF)
            hst[d] = jnp.exp(edge) * hs + lax.dot_general(xb, bw, TN, preferred_element_type=F32)
            yacc[rows, :] += y
        return carry

    lax.fori_loop(0, nc, body, 0, unroll=2)
    hf_ref[0, 0, 0] = hst[0]
    hf_ref[0, 1, 0] = hst[1]

    def epi(ri, carry):
        rows = pl.ds(pl.multiple_of(ri * 256, 256), 256)
        y = yacc[rows, :] + dpar_ref[0, h] * x_ref[rows, :].astype(F32)
        y_ref[rows, :] = (y * _silu(z_ref[rows, :].astype(F32))).astype(BF)
        return carry

    lax.fori_loop(0, T // 256, epi, 0)


def _ssd(xbc, z, dt_bias, a_log, dpar, h0, *, nseq, T, row0):
    smem = pl.BlockSpec(memory_space=pltpu.SMEM)
    return pl.pallas_call(
        functools.partial(_ssd_kernel, T=T),
        grid=(nseq, 8),
        in_specs=[smem, smem, smem,
                  pl.BlockSpec((T, 128), lambda s, h: (s, h)),
                  pl.BlockSpec((T, 128), lambda s, h: (s, 8 + h // 4)),
                  pl.BlockSpec((T, 128), lambda s, h: (s, 10 + h // 4)),
                  pl.BlockSpec((T, 128), lambda s, h: (row0 + s, Z_BZ // 128 + h)),
                  pl.BlockSpec((T, 128), lambda s, h: (row0 + s, Z_SM // 128)),
                  pl.BlockSpec((1, 2, 1, 128, 128), lambda s, h: (s, 0, h, 0, 0))],
        out_specs=[pl.BlockSpec((T, 128), lambda s, h: (s, h)),
                   pl.BlockSpec((1, 2, 1, 128, 128), lambda s, h: (s, 0, h, 0, 0))],
        out_shape=[jax.ShapeDtypeStruct((nseq * T, 1024), BF),
                   jax.ShapeDtypeStruct((nseq, 2, 8, 128, 128), F32)],
        scratch_shapes=[pltpu.VMEM((T, 128), F32), pltpu.VMEM((2, 128, 128), F32)],
        compiler_params=_cp("arbitrary", "arbitrary"),
        name=f"ssd_scan_T{T}",
    )(dt_bias, a_log, dpar, xbc, xbc, xbc, z, z, h0)


def _rms_rows_kernel(x_ref, g_ref, o_ref):
    x = x_ref[...].astype(F32)
    o_ref[...] = (x * lax.rsqrt(jnp.mean(x * x, axis=-1, keepdims=True) + EPS) * g_ref[...]).astype(BF)


def _rms_rows(x, g):
    tm = 512
    n, w = x.shape
    return pl.pallas_call(
        _rms_rows_kernel,
        grid=(n // tm,),
        in_specs=[pl.BlockSpec((tm, w), lambda i: (i, 0)), pl.BlockSpec((1, w), lambda i: (0, 0))],
        out_specs=pl.BlockSpec((tm, w), lambda i: (i, 0)),
        out_shape=jax.ShapeDtypeStruct((n, w), BF),
        compiler_params=_cp("arbitrary"),
        name="ssd_out_norm",
    )(x, g.reshape(1, w))


def _rope_tables():
    t = np.arange(T_LAT)
    row = (t // GRID_W).astype(np.float32)
    col = (t % GRID_W).astype(np.float32)
    freqs = (ROPE_THETA ** (-np.arange(16, dtype=np.float32) / 16)).astype(np.float32)
    ang_r = row[:, None] * freqs[None, :]
    ang_c = col[:, None] * freqs[None, :]
    ang = np.concatenate([ang_r, ang_r, ang_c, ang_c], axis=-1).astype(np.float32)
    ang = np.concatenate([ang, ang], axis=-1)
    return jnp.cos(jnp.asarray(ang)), jnp.sin(jnp.asarray(ang))


def _rope(x, cos, sin):
    lane = lax.broadcasted_iota(jnp.int32, x.shape, 1)
    even_quarter = ((lane // 16) % 2) == 0
    partner = jnp.where(even_quarter, -pltpu.roll(x, 112, 1), pltpu.roll(x, 16, 1))
    return x * cos + partner * sin


def _mla_kv_kernel(src_ref, sm_ref, cos_ref, sin_ref, wkvb_ref, gkva_ref, gkn_ref, gkp_ref,
                   kh_ref, vh_ref, ckv_ref, *, norm_kv, rope):
    cc = src_ref[...].astype(F32)
    if norm_kv:
        cc = cc * lax.rsqrt(jnp.mean(cc * cc, axis=-1, keepdims=True) + EPS) * gkva_ref[...]
    ckv_ref[...] = cc
    kv = jnp.dot(cc.astype(BF), wkvb_ref[...], preferred_element_type=F32)
    sm = sm_ref[...].astype(F32)
    lane = lax.broadcasted_iota(jnp.int32, sm.shape, 1)
    pe = jnp.where(lane < 64, pltpu.roll(sm, 64, 1), 0.0)
    pe2 = jnp.sum(pe * pe, axis=-1, keepdims=True)
    for h in range(8):
        kn = kv[:, h * 256:h * 256 + 128]
        ri = lax.rsqrt((jnp.sum(kn * kn, axis=-1, keepdims=True) + pe2) * (1.0 / 192.0) + EPS)
        kh_ref[:, h * 256:h * 256 + 128] = (kn * ri * gkn_ref[...]).astype(BF)
        p = pe * ri * gkp_ref[...]
        if rope:
            p = _rope(p, cos_ref[...], sin_ref[...])
        kh_ref[:, h * 256 + 128:(h + 1) * 256] = p.astype(BF)
        vh_ref[:, h * 128:(h + 1) * 128] = kv[:, h * 256 + 128:(h + 1) * 256].astype(BF)


def _mla_kv(src, src_blk, sm, sm_blk, cos, sin, wkvb, gkva, gkn, gkp, *, rows, norm_kv, rope, name):
    tm = 256
    ntab = T_LAT // tm
    return pl.pallas_call(
        functools.partial(_mla_kv_kernel, norm_kv=norm_kv, rope=rope),
        grid=(rows // tm,),
        in_specs=[pl.BlockSpec((tm, 256), lambda i: (src_blk[0] + i, src_blk[1])),
                  pl.BlockSpec((tm, 128), lambda i: (sm_blk[0] + i, sm_blk[1])),
                  pl.BlockSpec((tm, 128), lambda i: (i % ntab, 0)),
                  pl.BlockSpec((tm, 128), lambda i: (i % ntab, 0)),
                  pl.BlockSpec((256, 2048), lambda i: (0, 0)),
                  pl.BlockSpec((1, 256), lambda i: (0, 0)),
                  pl.BlockSpec((1, 128), lambda i: (0, 0)),
                  pl.BlockSpec((1, 128), lambda i: (0, 0))],
        out_specs=[pl.BlockSpec((tm, 2048), lambda i: (i, 0)),
                   pl.BlockSpec((tm, 1024), lambda i: (i, 0)),
                   pl.BlockSpec((tm, 256), lambda i: (i, 0))],
        out_shape=[jax.ShapeDtypeStruct((rows, 2048), BF),
                   jax.ShapeDtypeStruct((rows, 1024), BF),
                   jax.ShapeDtypeStruct((rows, 256), F32)],
        compiler_params=_cp("arbitrary"),
        name=name,
    )(src, sm, cos, sin, wkvb, gkva, gkn, gkp)


def _mla_q_kernel(qa_ref, cos_ref, sin_ref, wqb_ref, gqa_ref, gq_ref, qh_ref, *, rope):
    qa = qa_ref[...].astype(F32)
    qa = qa * lax.rsqrt(jnp.mean(qa * qa, axis=-1, keepdims=True) + EPS) * gqa_ref[...]
    q = jnp.dot(qa.astype(BF), wqb_ref[...], preferred_element_type=F32)
    scale = 192.0 ** -0.5 * LOG2E
    for h in range(8):
        qn = q[:, h * 256:h * 256 + 128]
        qp = q[:, h * 256 + 128:(h + 1) * 256]
        ss = jnp.sum(qn * qn, axis=-1, keepdims=True) + jnp.sum(qp * qp, axis=-1, keepdims=True)
        ri = lax.rsqrt(ss * (1.0 / 192.0) + EPS)
        qh_ref[:, h * 256:h * 256 + 128] = (qn * ri * gq_ref[:, 0:128] * scale).astype(BF)
        p = qp * ri * gq_ref[:, 128:256]
        if rope:
            p = _rope(p, cos_ref[...], sin_ref[...])
        qh_ref[:, h * 256 + 128:(h + 1) * 256] = (p * scale).astype(BF)


def _mla_q(z, row_blk0, cos, sin, wqb, gqa, gq, *, rows, rope, name):
    tm = 256
    ntab = T_LAT // tm
    return pl.pallas_call(
        functools.partial(_mla_q_kernel, rope=rope),
        grid=(rows // tm,),
        in_specs=[pl.BlockSpec((tm, 768), lambda i: (row_blk0 + i, Z_QA // 768)),
                  pl.BlockSpec((tm, 128), lambda i: (i % ntab, 0)),
                  pl.BlockSpec((tm, 128), lambda i: (i % ntab, 0)),
                  pl.BlockSpec((768, 2048), lambda i: (0, 0)),
                  pl.BlockSpec((1, 768), lambda i: (0, 0)),
                  pl.BlockSpec((1, 256), lambda i: (0, 0))],
        out_specs=pl.BlockSpec((tm, 2048), lambda i: (i, 0)),
        out_shape=jax.ShapeDtypeStruct((rows, 2048), BF),
        compiler_params=_cp("arbitrary"),
        name=name,
    )(z, cos, sin, wqb, gqa, gq)


def _mla_attn_kernel(q_ref, k_ref, v_ref, o_ref):
    s = lax.dot_general(q_ref[...], k_ref[...], NT, preferred_element_type=F32)
    p = jnp.exp2(s - jnp.max(s, axis=-1, keepdims=True))
    l = jnp.sum(p, axis=-1, keepdims=True)
    o = jnp.dot(p.astype(BF), v_ref[...], preferred_element_type=F32)
    o_ref[...] = (o / l).astype(BF)


def _mla_attn(q, k, v, *, nb, tq_total, tk, name):
    tq = 256
    nq = tq_total // tq
    return pl.pallas_call(
        _mla_attn_kernel,
        grid=(nb, 8, nq),
        in_specs=[pl.BlockSpec((tq, 256), lambda b, h, i: (b * nq + i, h)),
                  pl.BlockSpec((tk, 256), lambda b, h, i: (b, h)),
                  pl.BlockSpec((tk, 128), lambda b, h, i: (b, h))],
        out_specs=pl.BlockSpec((tq, 128), lambda b, h, i: (b * nq + i, h)),
        out_shape=jax.ShapeDtypeStruct((nb * tq_total, 1024), BF),
        compiler_params=_cp("arbitrary", "arbitrary", "arbitrary"),
        name=name,
    )(q, k, v)


def _group64_norm(x, g):
    r = lax.broadcasted_iota(jnp.int32, (128, 128), 0)
    c = lax.broadcasted_iota(jnp.int32, (128, 128), 1)
    gm = jnp.where((r // 64) == (c // 64), 1.0, 0.0).astype(BF)
    outs = []
    for j in range(8):
        xj = x[:, j * 128:(j + 1) * 128]
        sq = xj * xj
        hi = sq.astype(BF)
        lo = (sq - hi.astype(F32)).astype(BF)
        ms = (jnp.dot(hi, gm, preferred_element_type=F32) + jnp.dot(lo, gm, preferred_element_type=F32)) * (1.0 / 64.0)
        outs.append(xj * lax.rsqrt(ms + EPS) * g)
    return outs


def _diff_prep_kernel(q_ref, k_ref, cos_ref, sin_ref, gq_ref, gk_ref, qh_ref, kh_ref, kc_ref, *, rope):
    qs = _group64_norm(q_ref[...].astype(F32), gq_ref[...])
    ks = _group64_norm(k_ref[...].astype(F32), gk_ref[...])
    scale = 64.0 ** -0.5 * LOG2E
    for j in range(8):
        cols = slice(j * 128, (j + 1) * 128)
        qj, kj = qs[j], ks[j]
        kc_ref[:, cols] = kj
        if rope:
            qj = _rope(qj, cos_ref[...], sin_ref[...])
            kj = _rope(kj, cos_ref[...], sin_ref[...])
        qh_ref[:, cols] = (qj * scale).astype(BF)
        kh_ref[:, cols] = kj.astype(BF)


def _diff_prep(z, row_blk0, cos, sin, gq, gk, *, rows, rope, name):
    tm = 256
    ntab = T_LAT // tm
    return pl.pallas_call(
        functools.partial(_diff_prep_kernel, rope=rope),
        grid=(rows // tm,),
        in_specs=[pl.BlockSpec((tm, 1024), lambda i: (row_blk0 + i, Z_DQ // 1024)),
                  pl.BlockSpec((tm, 1024), lambda i: (row_blk0 + i, Z_DK // 1024)),
                  pl.BlockSpec((tm, 128), lambda i: (i % ntab, 0)),
                  pl.BlockSpec((tm, 128), lambda i: (i % ntab, 0)),
                  pl.BlockSpec((1, 128), lambda i: (0, 0)),
                  pl.BlockSpec((1, 128), lambda i: (0, 0))],
        out_specs=[pl.BlockSpec((tm, 1024), lambda i: (i, 0)),
                   pl.BlockSpec((tm, 1024), lambda i: (i, 0)),
                   pl.BlockSpec((tm, 1024), lambda i: (i, 0))],
        out_shape=[jax.ShapeDtypeStruct((rows, 1024), BF),
                   jax.ShapeDtypeStruct((rows, 1024), BF),
                   jax.ShapeDtypeStruct((rows, 1024), F32)],
        compiler_params=_cp("arbitrary"),
        name=name,
    )(z, z, cos, sin, gq, gk)


def _diff_attn_kernel(q_ref, k_ref, v_ref, lam_ref, g_ref, o_ref, *, lam_init):
    q = q_ref[...]
    k = k_ref[...]
    v = v_ref[...]
    lane = lax.broadcasted_iota(jnp.int32, q.shape, 1)
    zero = jnp.zeros_like(q)
    outs = []
    for m in range(2):
        qm = jnp.where((lane < 64) if m == 0 else (lane >= 64), q, zero)
        s = lax.dot_general(qm, k, NT, preferred_element_type=F32)
        p = jnp.exp2(s - jnp.max(s, axis=-1, keepdims=True))
        l = jnp.sum(p, axis=-1, keepdims=True)
        outs.append(jnp.dot(p.astype(BF), v, preferred_element_type=F32) / l)
    lam = lam_ref[...]
    lam_full = (jnp.exp(jnp.sum(lam[0:1, :] * lam[1:2, :], axis=-1, keepdims=True))
                - jnp.exp(jnp.sum(lam[2:3, :] * lam[3:4, :], axis=-1, keepdims=True)) + lam_init)
    od = outs[0] - lam_full * outs[1]
    y = od * lax.rsqrt(jnp.mean(od * od, axis=-1, keepdims=True) + EPS) * g_ref[...]
    o_ref[...] = (y * (1.0 - lam_init)).astype(BF)


def _diff_attn(q, k, v, v_col0, lam, g, *, nb, tq_total, tk, lam_init, name):
    tq = 256
    nq = tq_total // tq
    return pl.pallas_call(
        functools.partial(_diff_attn_kernel, lam_init=lam_init),
        grid=(nb, 8, nq),
        in_specs=[pl.BlockSpec((tq, 128), lambda b, h, i: (b * nq + i, h)),
                  pl.BlockSpec((tk, 128), lambda b, h, i: (b, h)),
                  pl.BlockSpec((tk, 128), lambda b, h, i: (b, v_col0 // 128 + h)),
                  pl.BlockSpec((4, 64), lambda b, h, i: (0, 0)),
                  pl.BlockSpec((1, 128), lambda b, h, i: (0, 0))],
        out_specs=pl.BlockSpec((tq, 128), lambda b, h, i: (b * nq + i, h)),
        out_shape=jax.ShapeDtypeStruct((nb * tq_total, 1024), BF),
        compiler_params=_cp("arbitrary", "arbitrary", "arbitrary"),
        name=name,
    )(q, k, v, lam, g)


PEER_TOPK = 16
_CAND_GROUPS = [(0, 16), (1, 8), (2, 5), (3, 4), (4, 3), (5, 2), (6, 2), (7, 2)]


def _top16_rows(s, payload=None):
    nrow = s.shape[0]
    row = lax.broadcasted_iota(jnp.int32, s.shape, 0)
    vals, idxs, pays = [], [], [[] for _ in (payload or [])]
    for _ in range(PEER_TOPK):
        m = jnp.max(s, axis=0, keepdims=True)
        i = jnp.min(jnp.where(s == m, row, nrow), axis=0, keepdims=True)
        hit = row == i
        vals.append(m)
        idxs.append(i)
        for a, pay in enumerate(payload or []):
            pays[a].append(jnp.max(jnp.where(hit, pay, -1), axis=0, keepdims=True))
        s = jnp.where(hit, -jnp.inf, s)
    return (jnp.concatenate(vals, axis=0), jnp.concatenate(idxs, axis=0),
            [jnp.concatenate(p, axis=0) for p in pays])


def _peer_route_kernel(q_ref, keys_ref, w_ref, a_s, b_s, g_s, *, tn):
    sub = lax.broadcasted_iota(jnp.int32, (8, tn), 0)
    for h in range(8):
        tops = []
        for p in range(2):
            qhp = q_ref[:, (h * 2 + p) * 128:(h * 2 + p + 1) * 128]
            st = lax.dot_general(keys_ref[h, p], qhp, NT, preferred_element_type=F32)
            v, i, _ = _top16_rows(st)
            tops.append((v, i))
        (s1, i1), (s2, i2) = tops
        cs, ca, cb = [], [], []
        for r, nvalid in _CAND_GROUPS:
            width = 16 if r == 0 else 8
            blk = s1[r:r + 1, :] + s2[0:width, :]
            if nvalid < width:
                blk = jnp.where(sub < nvalid, blk, -jnp.inf)
            cs.append(blk)
            ca.append(jnp.broadcast_to(i1[r:r + 1, :], (width, tn)))
            cb.append(i2[0:width, :])
        cs.append(s1[8:16, :] + s2[0:1, :])
        ca.append(i1[8:16, :])
        cb.append(jnp.broadcast_to(i2[0:1, :], (8, tn)))
        cand_s = jnp.concatenate(cs, axis=0)
        cand_a = jnp.concatenate(ca, axis=0)
        cand_b = jnp.concatenate(cb, axis=0)
        top_s, _, (sel_a, sel_b) = _top16_rows(cand_s, [cand_a, cand_b])
        e = jnp.exp(top_s - top_s[0:1, :])
        g = e / jnp.sum(e, axis=0, keepdims=True)
        a_s[:, h * 16:(h + 1) * 16] = sel_a.astype(F32).T
        b_s[:, h * 16:(h + 1) * 16] = sel_b.astype(F32).T
        g_s[:, h * 16:(h + 1) * 16] = g.T

    ids = lax.broadcasted_iota(jnp.int32, (128, 128), 0).astype(F32)

    def build(t, carry):
        n0 = pl.multiple_of(t * 16, 16)
        a_rows = a_s[pl.ds(n0, 16), :]
        b_rows = b_s[pl.ds(n0, 16), :]
        g_rows = g_s[pl.ds(n0, 16), :]
        ws = []
        for r in range(16):
            pt = jnp.where(ids == a_rows[r:r + 1, :], g_rows[r:r + 1, :], 0.0).astype(BF)
            qt = jnp.where(ids == b_rows[r:r + 1, :], 1.0, 0.0).astype(BF)
            ws.append(lax.dot_general(pt, qt, NT, preferred_element_type=F32))
        w_ref[:, pl.ds(n0, 16), :] = jnp.swapaxes(jnp.stack(ws, axis=0), 0, 1).astype(BF)
        return carry

    lax.fori_loop(0, tn // 16, build, 0)


def _peer_route(q, keys, layer):
    tn = 128
    n = q.shape[0]
    return pl.pallas_call(
        functools.partial(_peer_route_kernel, tn=tn),
        grid=(n // tn,),
        in_specs=[pl.BlockSpec((tn, 2048), lambda i: (i, 0)),
                  pl.BlockSpec((None, 8, 2, 128, 128), lambda i: (layer, 0, 0, 0, 0))],
        out_specs=pl.BlockSpec((128, tn, 128), lambda i: (0, i, 0)),
        out_shape=jax.ShapeDtypeStruct((128, n, 128), BF),
        scratch_shapes=[pltpu.VMEM((tn, 128), F32), pltpu.VMEM((tn, 128), F32), pltpu.VMEM((tn, 128), F32)],
        compiler_params=_cp("arbitrary"),
        name="peer_route",
    )(q, keys)


def _gelu_tanh(x):
    return 0.5 * x * (1.0 + jnp.tanh(0.7978845608028654 * (x + 0.044715 * x * x * x)))


def _peer_dense_kernel(x_ref, u_ref, v_ref, w_ref, o_ref):
    @pl.when(pl.program_id(1) == 0)
    def _():
        o_ref[...] = jnp.zeros_like(o_ref)

    hid = lax.dot_general(x_ref[...], u_ref[...], NT, preferred_element_type=F32)
    w = jnp.concatenate([w_ref[a] for a in range(w_ref.shape[0])], axis=1)
    act = (_gelu_tanh(hid) * w.astype(F32)).astype(BF)
    o_ref[...] += jnp.dot(act, v_ref[...], preferred_element_type=F32)


def _peer_dense(x, u, v, w, layer):
    tn, te = 512, 512
    n = x.shape[0]
    ne = u.shape[1]
    return pl.pallas_call(
        _peer_dense_kernel,
        grid=(n // tn, ne // te),
        in_specs=[pl.BlockSpec((tn, D), lambda i, e: (i, 0), pipeline_mode=pl.Buffered(1)),
                  pl.BlockSpec((None, te, D), lambda i, e: (layer, e, 0)),
                  pl.BlockSpec((None, te, D), lambda i, e: (layer, e, 0)),
                  pl.BlockSpec((te // 128, tn, 128), lambda i, e: (e, i, 0))],
        out_specs=pl.BlockSpec((tn, D), lambda i, e: (i, 0), pipeline_mode=pl.Buffered(1)),
        out_shape=jax.ShapeDtypeStruct((n, D), F32),
        compiler_params=_cp("arbitrary", "arbitrary"),
        name="peer_dense",
    )(x, u, v, w)


def _permute_w_in(w):
    o = dict(a_q=0, a_k=512, a_v=1024, a_g=2048, a_gk=3072, b_z=3104, b_xbc=4128, b_dt=5664,
             c_qa=5680, c_kva=6448, c_kpe=6704, d_q=6768, d_k=7792, d_v=8816)

    def seg(name, width):
        return w[..., o[name]:o[name] + width]

    zeros = lambda n: jnp.zeros(w.shape[:-1] + (n,), w.dtype)
    parts = [seg('a_v', 1024), seg('a_g', 1024), seg('b_z', 1024), seg('d_q', 1024), seg('d_k', 1024),
             seg('d_v', 1024), seg('b_xbc', 1536), seg('a_q', 512), seg('a_k', 512), seg('c_kva', 256),
             seg('a_gk', 32), seg('b_dt', 16), zeros(16), seg('c_kpe', 64), zeros(128), seg('c_qa', 768)]
    return jnp.concatenate(parts, axis=-1).astype(BF)


def _permute_w_qb(w):
    w3 = w.reshape(w.shape[0], 8, 192)
    return jnp.concatenate([w3, jnp.zeros((w.shape[0], 8, 64), w.dtype)], axis=-1).reshape(w.shape[0], 2048).astype(BF)


def kernel(x_prompt, x_sample, cache_mla_ckv, cache_mla_kpe, cache_diff_k, cache_diff_v, state_gla, state_ssd, c, c_ctx, norm1_g, norm2_g, w_ada, b_ada, w_in, w_out, gla_w_gk, gla_b_gk, gla_norm_g, ssd_conv_w, ssd_conv_b, ssd_a_log, ssd_dt_bias, ssd_d, ssd_norm_g, mla_qa_norm_g, mla_w_qb, mla_kva_norm_g, mla_w_kvb, mla_q_norm_g, mla_k_norm_g, diff_q_norm_g, diff_k_norm_g, diff_lambda, diff_subln_g, peer_w_q, peer_sub_keys, peer_u, peer_v):
    depth = w_in.shape[0]
    w_in_b = _permute_w_in(w_in)
    w_out_b = w_out.astype(BF)
    peer_wq_b = peer_w_q.astype(BF)
    peer_keys_b = peer_sub_keys.astype(BF)
    peer_u_b = peer_u.astype(BF)
    peer_v_b = peer_v.astype(BF)
    x = jnp.concatenate([x_prompt.reshape(N_CTX, D), x_sample.reshape(N_LAT, D)], axis=0)
    cv8 = jnp.concatenate([c_ctx[None, :], c, jnp.zeros((5, D), F32)], axis=0)
    cos_t, sin_t = _rope_tables()
    nct = N_CTX // T_CTX
    outs = {k: [] for k in ('ckv', 'kpe', 'dk', 'dv', 'gla', 'ssd')}

    for l in range(depth):
        mod3 = _modulation(cv8, w_ada, b_ada, l).reshape(8, 6, D)[:3]

        (z,) = _norm_mm(x, norm1_g[l], mod3, w_in_b, l, shift_row=0, tn=768, emit_xn=False, name="norm_in_proj")

        wgk = gla_w_gk[l]
        wgk_pad = jnp.zeros((2, 128, 512), F32)
        wgk_pad = wgk_pad.at[0, SM_GK:SM_GK + 16].set(wgk[0]).at[1, SM_GK + 16:SM_GK + 32].set(wgk[1]).astype(BF)
        bgk = gla_b_gk[l].reshape(2, 1, 512)
        ng = gla_norm_g[l].reshape(1, 128)
        a_ctx, gla_fin = _gla(z, wgk_pad, bgk, ng, jnp.zeros((B_CTX, 2, 8, 64, 128), F32),
                              nseq=B_CTX, T=T_CTX, row0=0)
        a_lat, _ = _gla(z, wgk_pad, bgk, ng, state_gla[:, l], nseq=B_LAT, T=T_LAT, row0=N_CTX // T_LAT)
        out_a = jnp.concatenate([a_ctx, a_lat], axis=0)

        conv_b = ssd_conv_b[l].reshape(1, 1536)
        xbc_ctx = _conv(z, ssd_conv_w[l], conv_b, nseq=B_CTX, T=T_CTX, row0=0)
        xbc_lat = _conv(z, ssd_conv_w[l], conv_b, nseq=B_LAT, T=T_LAT, row0=N_CTX // T_LAT)
        dpar = ssd_d[l].reshape(1, 8)
        y_ctx, ssd_fin = _ssd(xbc_ctx, z, ssd_dt_bias[l], ssd_a_log[l], dpar,
                              jnp.zeros((B_CTX, 2, 8, 128, 128), F32), nseq=B_CTX, T=T_CTX, row0=0)
        y_lat, _ = _ssd(xbc_lat, z, ssd_dt_bias[l], ssd_a_log[l], dpar, state_ssd[:, l],
                        nseq=B_LAT, T=T_LAT, row0=N_CTX // T_LAT)
        out_b = _rms_rows(jnp.concatenate([y_ctx, y_lat], axis=0), ssd_norm_g[l])

        wkvb = mla_w_kvb[l].astype(BF)
        wqb = _permute_w_qb(mla_w_qb[l])
        gkva = mla_kva_norm_g[l].reshape(1, 256)
        gk = mla_k_norm_g[l]
        gkn = gk[:128].reshape(1, 128)
        gkp = jnp.concatenate([gk[128:], jnp.zeros((64,), F32)]).reshape(1, 128)
        gq = jnp.concatenate([mla_q_norm_g[l], jnp.zeros((64,), F32)]).reshape(1, 256)
        gqa = mla_qa_norm_g[l].reshape(1, 768)
        kh_ctx, vh_ctx, ckv_ctx = _mla_kv(z, (0, Z_KVA // 256), z, (0, Z_SM // 128), cos_t, sin_t, wkvb, gkva, gkn, gkp,
                                          rows=N_CTX, norm_kv=True, rope=False, name="mla_kv_ctx")
        kh_lat, vh_lat, _ = _mla_kv(z, (N_CTX // 256, Z_KVA // 256), z, (N_CTX // 256, Z_SM // 128), cos_t, sin_t,
                                    wkvb, gkva, gkn, gkp, rows=N_LAT, norm_kv=True, rope=True, name="mla_kv_lat")
        cache_sm = jnp.concatenate([jnp.zeros((B_LAT * PAST, 64), F32),
                                    cache_mla_kpe[:, l].reshape(B_LAT * PAST, 64)], axis=1)
        kh_past, vh_past, _ = _mla_kv(cache_mla_ckv[:, l].reshape(B_LAT * PAST, 256), (0, 0), cache_sm, (0, 0),
                                      cos_t, sin_t, wkvb, gkva, gkn, gkp,
                                      rows=B_LAT * PAST, norm_kv=False, rope=False, name="mla_kv_past")
        qh_ctx = _mla_q(z, 0, cos_t, sin_t, wqb, gqa, gq, rows=N_CTX, rope=False, name="mla_q_ctx")
        qh_lat = _mla_q(z, N_CTX // 256, cos_t, sin_t, wqb, gqa, gq, rows=N_LAT, rope=True, name="mla_q_lat")
        tk_lat = PAST + T_LAT
        k_lat = jnp.concatenate([kh_past.reshape(B_LAT, PAST, 2048), kh_lat.reshape(B_LAT, T_LAT, 2048)],
                                axis=1).reshape(B_LAT * tk_lat, 2048)
        v_lat = jnp.concatenate([vh_past.reshape(B_LAT, PAST, 1024), vh_lat.reshape(B_LAT, T_LAT, 1024)],
                                axis=1).reshape(B_LAT * tk_lat, 1024)
        c_ctx_o = _mla_attn(qh_ctx, kh_ctx, vh_ctx, nb=B_CTX, tq_total=T_CTX, tk=T_CTX, name="mla_attn_ctx")
        c_lat_o = _mla_attn(qh_lat, k_lat, v_lat, nb=B_LAT, tq_total=T_LAT, tk=tk_lat, name="mla_attn_lat")
        out_c = jnp.concatenate([c_ctx_o, c_lat_o], axis=0)

        gdq = jnp.tile(diff_q_norm_g[l], 2).reshape(1, 128)
        gdk = jnp.tile(diff_k_norm_g[l], 2).reshape(1, 128)
        dq_ctx, dk_ctx, dkc_ctx = _diff_prep(z, 0, cos_t, sin_t, gdq, gdk, rows=N_CTX, rope=False, name="diff_prep_ctx")
        dq_lat, dk_lat, _ = _diff_prep(z, N_CTX // 256, cos_t, sin_t, gdq, gdk, rows=N_LAT, rope=True,
                                       name="diff_prep_lat")
        dk_full = jnp.concatenate([cache_diff_k[:, l].reshape(B_LAT, PAST, 1024).astype(BF),
                                   dk_lat.reshape(B_LAT, T_LAT, 1024)], axis=1).reshape(B_LAT * tk_lat, 1024)
        dv_full = jnp.concatenate([cache_diff_v[:, l].reshape(B_LAT, PAST, 1024).astype(BF),
                                   z[N_CTX:, Z_DV:Z_DV + 1024].reshape(B_LAT, T_LAT, 1024)],
                                  axis=1).reshape(B_LAT * tk_lat, 1024)
        lam_init = 0.8 - 0.6 * math.exp(-0.3 * l)
        gsub = diff_subln_g[l].reshape(1, 128)
        d_ctx_o = _diff_attn(dq_ctx, dk_ctx, z, Z_DV, diff_lambda[l], gsub, nb=B_CTX, tq_total=T_CTX, tk=T_CTX,
                             lam_init=lam_init, name="diff_attn_ctx")
        d_lat_o = _diff_attn(dq_lat, dk_full, dv_full, 0, diff_lambda[l], gsub, nb=B_LAT, tq_total=T_LAT, tk=tk_lat,
                             lam_init=lam_init, name="diff_attn_lat")
        out_d = jnp.concatenate([d_ctx_o, d_lat_o], axis=0)

        hmid = _out_proj((out_a, out_b, out_c, out_d), w_out_b, l, x, mod3)

        pq, u2 = _norm_mm(hmid, norm2_g[l], mod3, peer_wq_b, l, shift_row=3, tn=512, emit_xn=True,
                          name="norm_peer_query")
        wmap = _peer_route(pq, peer_keys_b, l)
        pe_out = _peer_dense(u2, peer_u_b, peer_v_b, wmap, l)
        if l + 1 < depth:
            x = _resid(hmid, pe_out, mod3)
        else:
            y_p = _resid(hmid, pe_out, mod3, 0, N_CTX).reshape(B_CTX, T_CTX, D)
            y_s = _resid(hmid, pe_out, mod3, N_CTX, N_LAT).reshape(B_LAT, T_LAT, D)

        outs['ckv'].append(ckv_ctx.reshape(B_CTX, T_CTX, 256))
        outs['kpe'].append(z[:N_CTX, Z_SM + SM_KPE:Z_SM + 128].astype(F32).reshape(B_CTX, T_CTX, 64))
        outs['dk'].append(dkc_ctx.reshape(B_CTX, T_CTX, 8, 2, 64))
        outs['dv'].append(z[:N_CTX, Z_DV:Z_DV + 1024].astype(F32).reshape(B_CTX, T_CTX, 8, 128))
        outs['gla'].append(gla_fin)
        outs['ssd'].append(ssd_fin)

    return (y_p, y_s, jnp.stack(outs['ckv'], axis=1), jnp.stack(outs['kpe'], axis=1),
            jnp.stack(outs['dk'], axis=1), jnp.stack(outs['dv'], axis=1),
            jnp.stack(outs['gla'], axis=1), jnp.stack(outs['ssd'], axis=1))
```

```python
import functools
import math

import jax
import jax.numpy as jnp
import numpy as np
from jax import lax
from jax.experimental import pallas as pl
from jax.experimental.pallas import tpu as pltpu

F32 = jnp.float32
BF = jnp.bfloat16
LOG2E = 1.4426950408889634

D = 4096
EPS = 1e-6
N_CTX = 4096
N_LAT = 8192
N_TOK = N_CTX + N_LAT
MOD_ROWS = 4096
T_CTX, B_CTX = 256, 16
T_LAT, B_LAT = 4096, 2
PAST = 512
GRID_W = 64
ROPE_THETA = 10000.0

V7X_VMEM_LIMIT = 56 * 1024 * 1024

Z_AV, Z_AG, Z_BZ, Z_DQ, Z_DK, Z_DV = 0, 1024, 2048, 3072, 4096, 5120
Z_XBC, Z_AQ, Z_AK, Z_KVA, Z_SM, Z_QA = 6144, 7680, 8192, 8704, 8960, 9216
Z_W = 9984
SM_GK, SM_DT, SM_KPE = 0, 32, 64

NT = (((1,), (1,)), ((), ()))
TN = (((0,), (0,)), ((), ()))


def _cp(*sem):
    return pltpu.CompilerParams(dimension_semantics=sem, vmem_limit_bytes=V7X_VMEM_LIMIT)


def _sigmoid(x):
    return 1.0 / (1.0 + jnp.exp(-x))


def _silu(x):
    return x * _sigmoid(x)


def _softplus(x):
    return jnp.maximum(x, 0.0) + jnp.log1p(jnp.exp(-jnp.abs(x)))


def _log_sigmoid(x):
    return jnp.minimum(x, 0.0) - jnp.log1p(jnp.exp(-jnp.abs(x)))


def _block_cumsum(x, blk, reverse):
    n = x.shape[0]
    row = lax.broadcasted_iota(jnp.int32, x.shape, 0) % blk
    step = 1
    while step < blk:
        if reverse:
            x = x + jnp.where(row < blk - step, pltpu.roll(x, n - step, 0), 0.0)
        else:
            x = x + jnp.where(row >= step, pltpu.roll(x, step, 0), 0.0)
        step *= 2
    return x


def _mod_kernel(c_ref, w_ref, b_ref, o_ref):
    a = _silu(c_ref[...])
    o_ref[...] = jnp.dot(a.astype(BF), w_ref[...].astype(BF), preferred_element_type=F32) + b_ref[...]


def _modulation(cv8, w_ada, b_ada, layer):
    tn = 512
    return pl.pallas_call(
        _mod_kernel,
        grid=(6 * D // tn,),
        in_specs=[pl.BlockSpec((8, D), lambda j: (0, 0)),
                  pl.BlockSpec((None, D, tn), lambda j: (layer, 0, j)),
                  pl.BlockSpec((None, 1, tn), lambda j: (layer, 0, j))],
        out_specs=pl.BlockSpec((8, tn), lambda j: (0, j)),
        out_shape=jax.ShapeDtypeStruct((8, 6 * D), F32),
        compiler_params=_cp("arbitrary"),
        name="modulation",
    )(cv8, w_ada, b_ada.reshape(b_ada.shape[0], 1, 6 * D))


def _norm_mm_kernel(x_ref, g_ref, mod_ref, w_ref, o_ref, *rest, shift_row, emit_xn):
    if emit_xn:
        xn_out_ref, xn = rest
    else:
        (xn,) = rest

    @pl.when(pl.program_id(1) == 0)
    def _():
        m = mod_ref[0]
        gain = g_ref[...] * (1.0 + m[shift_row + 1:shift_row + 2, :])
        shift = m[shift_row:shift_row + 1, :]

        def chunk(r, carry):
            rows = pl.ds(pl.multiple_of(r * 64, 64), 64)
            x = x_ref[rows, :]
            u = (x * lax.rsqrt(jnp.mean(x * x, axis=-1, keepdims=True) + EPS) * gain + shift).astype(BF)
            xn[rows, :] = u
            if emit_xn:
                xn_out_ref[rows, :] = u
            return carry

        lax.fori_loop(0, x_ref.shape[0] // 64, chunk, 0)

    o_ref[...] = jnp.dot(xn[...], w_ref[...], preferred_element_type=F32).astype(o_ref.dtype)


def _norm_mm(x, g, mod3, w, layer, *, shift_row, tn, emit_xn, name):
    tm = 512
    n, k = x.shape
    nout = w.shape[2]
    out_shape = [jax.ShapeDtypeStruct((n, nout), BF)]
    out_specs = [pl.BlockSpec((tm, tn), lambda i, j: (i, j))]
    if emit_xn:
        out_shape.append(jax.ShapeDtypeStruct((n, k), BF))
        out_specs.append(pl.BlockSpec((tm, k), lambda i, j: (i, 0)))
    res = pl.pallas_call(
        functools.partial(_norm_mm_kernel, shift_row=shift_row, emit_xn=emit_xn),
        grid=(n // tm, nout // tn),
        in_specs=[pl.BlockSpec((tm, k), lambda i, j: (i, 0)),
                  pl.BlockSpec((1, k), lambda i, j: (0, 0)),
                  pl.BlockSpec((1, 6, k), lambda i, j: (i * tm // MOD_ROWS, 0, 0)),
                  pl.BlockSpec((None, k, tn), lambda i, j: (layer, 0, j))],
        out_specs=out_specs,
        out_shape=out_shape,
        scratch_shapes=[pltpu.VMEM((tm, k), BF)],
        compiler_params=_cp("arbitrary", "arbitrary"),
        name=name,
    )(x, g.reshape(1, k), mod3, w)
    return res


def _out_proj_kernel(a_ref, b_ref, c_ref, d_ref, w_ref, x_ref, mod_ref, o_ref):
    acc = jnp.dot(a_ref[...], w_ref[0:1024, :], preferred_element_type=F32)
    acc += jnp.dot(b_ref[...], w_ref[1024:2048, :], preferred_element_type=F32)
    acc += jnp.dot(c_ref[...], w_ref[2048:3072, :], preferred_element_type=F32)
    acc += jnp.dot(d_ref[...], w_ref[3072:4096, :], preferred_element_type=F32)
    o_ref[...] = x_ref[...] + mod_ref[0][2:3, :] * acc


def _out_proj(mix, w, layer, x, mod3):
    tm, tn = 512, 1024
    n = x.shape[0]
    mspec = pl.BlockSpec((tm, 1024), lambda i, j: (i, 0))
    return pl.pallas_call(
        _out_proj_kernel,
        grid=(n // tm, D // tn),
        in_specs=[mspec, mspec, mspec, mspec,
                  pl.BlockSpec((None, D, tn), lambda i, j: (layer, 0, j)),
                  pl.BlockSpec((tm, tn), lambda i, j: (i, j)),
                  pl.BlockSpec((1, 6, tn), lambda i, j: (i * tm // MOD_ROWS, 0, j))],
        out_specs=pl.BlockSpec((tm, tn), lambda i, j: (i, j)),
        out_shape=jax.ShapeDtypeStruct((n, D), F32),
        compiler_params=_cp("arbitrary", "arbitrary"),
        name="out_proj",
    )(*mix, w, x, mod3)


GLA_BLK = 16


def _gla_kernel(q_ref, k_ref, v_ref, g_ref, sm_ref, wgk_ref, bgk_ref, ng_ref, s0_ref,
                o_ref, sf_ref, oacc, cumf, cumb, st, *, T):
    nb = T // GLA_BLK

    def gate_body(r, carry):
        rows = pl.ds(pl.multiple_of(r * 256, 256), 256)
        smb = sm_ref[rows, :]
        gf = jnp.dot(smb, wgk_ref[0], preferred_element_type=F32) + bgk_ref[0]
        gb = jnp.dot(smb, wgk_ref[1], preferred_element_type=F32) + bgk_ref[1]
        laf = _log_sigmoid(gf) * (1.0 / 16.0)
        lab = _log_sigmoid(gb) * (1.0 / 16.0)
        cumf[rows, :] = _block_cumsum(laf, GLA_BLK, reverse=False)
        cumb[rows, :] = _block_cumsum(lab, GLA_BLK, reverse=True)
        oacc[rows, :] = jnp.zeros((256, 256), F32)
        return carry

    lax.fori_loop(0, T // 256, gate_body, 0)

    for d in range(2):
        for h in range(2):
            st[d, h] = s0_ref[0, d, h].T

    scale = 64.0 ** -0.5
    rowi = lax.broadcasted_iota(jnp.int32, (GLA_BLK, 128), 0)
    sel_r = lax.broadcasted_iota(jnp.int32, (128, 256), 0)
    sel_c = lax.broadcasted_iota(jnp.int32, (128, 256), 1)
    head_sel = jnp.where((sel_r // 64) == (sel_c // 128), 1.0, 0.0).astype(BF)

    def body(i, carry):
        for d in range(2):
            blk = i if d == 0 else nb - 1 - i
            rows = pl.ds(pl.multiple_of(blk * GLA_BLK, GLA_BLK), GLA_BLK)
            qb = q_ref[rows, :].astype(F32) * scale
            kb = k_ref[rows, :].astype(F32)
            vb = v_ref[rows, :]
            vf = vb.astype(F32)
            cb = cumf[rows, :] if d == 0 else cumb[rows, :]
            edge = cb[GLA_BLK - 1:GLA_BLK, :] if d == 0 else cb[0:1, :]
            qh = qb * jnp.exp(cb)
            kt = kb * jnp.exp(edge - cb)
            dec = jnp.exp(edge)
            ws = []
            for s in range(GLA_BLK):
                msk = (rowi >= s) if d == 0 else (rowi <= s)
                w = qb * kb[s:s + 1, :] * jnp.exp(jnp.minimum(cb - cb[s:s + 1, :], 0.0))
                ws.append(jnp.where(msk, w, 0.0))
            wst = jnp.concatenate(ws, axis=0).astype(BF)
            att = jnp.dot(wst, head_sel, preferred_element_type=F32)
            od = jnp.zeros((GLA_BLK, 256), F32)
            for s in range(GLA_BLK):
                od = od + att[s * GLA_BLK:(s + 1) * GLA_BLK, :] * vf[s:s + 1, :]
            od0 = od[:, 0:128]
            od1 = od[:, 128:256]
            for h in range(2):
                hs = slice(h * 64, h * 64 + 64)
                s_t = st[d, h]
                o_h = lax.dot_general(qh[:, hs].astype(BF), s_t.astype(BF), NT, preferred_element_type=F32)
                o_h = o_h + (od0 if h == 0 else od1)
                upd = lax.dot_general(vb[:, h * 128:(h + 1) * 128], kt[:, hs].astype(BF), TN,
                                      preferred_element_type=F32)
                st[d, h] = s_t * dec[:, hs] + upd
                oacc[rows, h * 128:(h + 1) * 128] += o_h
        return carry

    lax.fori_loop(0, nb, body, 0, unroll=2)

    for d in range(2):
        for h in range(2):
            sf_ref[0, d, h] = st[d, h].T

    def epi(r, carry):
        rows = pl.ds(pl.multiple_of(r * 256, 256), 256)
        for h in range(2):
            cols = slice(h * 128, (h + 1) * 128)
            o = oacc[rows, cols]
            y = o * lax.rsqrt(jnp.mean(o * o, axis=-1, keepdims=True) + EPS) * ng_ref[...]
            o_ref[rows, cols] = (y * _silu(g_ref[rows, cols].astype(F32))).astype(BF)
        return carry

    lax.fori_loop(0, T // 256, epi, 0)


def _gla(z, wgk_pad, bgk, ng, s0, *, nseq, T, row0):
    return pl.pallas_call(
        functools.partial(_gla_kernel, T=T),
        grid=(nseq, 4),
        in_specs=[pl.BlockSpec((T, 128), lambda s, p: (row0 + s, Z_AQ // 128 + p)),
                  pl.BlockSpec((T, 128), lambda s, p: (row0 + s, Z_AK // 128 + p)),
                  pl.BlockSpec((T, 256), lambda s, p: (row0 + s, Z_AV // 256 + p)),
                  pl.BlockSpec((T, 256), lambda s, p: (row0 + s, Z_AG // 256 + p)),
                  pl.BlockSpec((T, 128), lambda s, p: (row0 + s, Z_SM // 128)),
                  pl.BlockSpec((2, 128, 128), lambda s, p: (0, 0, p)),
                  pl.BlockSpec((2, 1, 128), lambda s, p: (0, 0, p)),
                  pl.BlockSpec((1, 128), lambda s, p: (0, 0)),
                  pl.BlockSpec((1, 2, 2, 64, 128), lambda s, p: (s, 0, p, 0, 0))],
        out_specs=[pl.BlockSpec((T, 256), lambda s, p: (s, p)),
                   pl.BlockSpec((1, 2, 2, 64, 128), lambda s, p: (s, 0, p, 0, 0))],
        out_shape=[jax.ShapeDtypeStruct((nseq * T, 1024), BF),
                   jax.ShapeDtypeStruct((nseq, 2, 8, 64, 128), F32)],
        scratch_shapes=[pltpu.VMEM((T, 256), F32), pltpu.VMEM((T, 128), F32), pltpu.VMEM((T, 128), F32),
                        pltpu.VMEM((2, 2, 128, 64), F32)],
        compiler_params=_cp("arbitrary", "arbitrary"),
        name=f"gla_T{T}",
    )(z, z, z, z, z, wgk_pad, bgk, ng, s0)


def _conv_kernel(x_ref, w_ref, b_ref, o_ref, xp, *, T):
    xp[0:8, :] = jnp.zeros((8, 256), F32)
    xp[8 + T:16 + T, :] = jnp.zeros((8, 256), F32)
    xp[8:8 + T, :] = x_ref[...].astype(F32)
    for r in range(T // 256):
        acc = b_ref[...] + w_ref[0:1, :] * xp[6 + r * 256:6 + (r + 1) * 256, :]
        for kk in range(1, 5):
            acc = acc + w_ref[kk:kk + 1, :] * xp[6 + kk + r * 256:6 + kk + (r + 1) * 256, :]
        o_ref[r * 256:(r + 1) * 256, :] = _silu(acc).astype(BF)


def _conv(z, w, b, *, nseq, T, row0):
    return pl.pallas_call(
        functools.partial(_conv_kernel, T=T),
        grid=(nseq, 6),
        in_specs=[pl.BlockSpec((T, 256), lambda s, j: (row0 + s, Z_XBC // 256 + j)),
                  pl.BlockSpec((5, 256), lambda s, j: (0, j)),
                  pl.BlockSpec((1, 256), lambda s, j: (0, j))],
        out_specs=pl.BlockSpec((T, 256), lambda s, j: (s, j)),
        out_shape=jax.ShapeDtypeStruct((nseq * T, 1536), BF),
        scratch_shapes=[pltpu.VMEM((T + 16, 256), F32)],
        compiler_params=_cp("arbitrary", "arbitrary"),
        name=f"ssd_conv_T{T}",
    )(z, w, b)


SSD_CHUNK = 128


def _ssd_kernel(bias_ref, alog_ref, dpar_ref, x_ref, b_ref, c_ref, z_ref, sm_ref, h0_ref,
                y_ref, hf_ref, yacc, dts, cums, hst, *, T):
    h = pl.program_id(1)
    nc = T // SSD_CHUNK
    r = lax.broadcasted_iota(jnp.int32, (128, 128), 0)
    c = lax.broadcasted_iota(jnp.int32, (128, 128), 1)
    hst[0] = h0_ref[0, 0, 0]
    hst[1] = h0_ref[0, 1, 0]

    def prep(ci, carry):
        rows = pl.ds(pl.multiple_of(ci * SSD_CHUNK, SSD_CHUNK), SSD_CHUNK)
        yacc[rows, :] = jnp.zeros((SSD_CHUNK, 128), F32)
        for d in range(2):
            sel = jnp.where(r == SM_DT + d * 8 + h, 1.0, 0.0).astype(BF)
            raw = jnp.dot(sm_ref[rows, :], sel, preferred_element_type=F32)
            dtb = _softplus(raw + bias_ref[d, h])
            a_neg = -jnp.exp(jnp.full((1, 128), alog_ref[d, h], F32))
            dts[d, rows, :] = dtb
            cums[d, rows, :] = _block_cumsum(dtb * a_neg, SSD_CHUNK, reverse=(d == 1))
        return carry

    lax.fori_loop(0, nc, prep, 0, unroll=4 if nc % 4 == 0 else 2)

    def body(ci, carry):
        for d in range(2):
            blk = ci if d == 0 else nc - 1 - ci
            rows = pl.ds(pl.multiple_of(blk * SSD_CHUNK, SSD_CHUNK), SSD_CHUNK)
            dtb = dts[d, rows, :]
            cb = cums[d, rows, :]
            c_t = cb.T
            dt_t = dtb.T
            msk = (r >= c) if d == 0 else (r <= c)
            seg = jnp.exp(jnp.where(msk, cb - c_t, -jnp.inf))
            cm = c_ref[rows, :]
            bm = b_ref[rows, :]
            xb = x_ref[rows, :]
            scores = lax.dot_general(cm, bm, NT, preferred_element_type=F32) * seg * dt_t
            hs = hst[d]
            y = jnp.dot(scores.astype(BF), xb, preferred_element_type=F32)
            y = y + lax.dot_general(cm, hs.astype(BF), NT, preferred_element_type=F32) * jnp.exp(cb)
            edge = cb[SSD_CHUNK - 1:SSD_CHUNK, :] if d == 0 else cb[0:1, :]
            wgt = jnp.exp(edge - cb) * dtb
            bw = (bm.astype(F32) * wgt).astype(BF)
            hst[d] = jnp.exp(edge) * hs + lax.dot_general(xb, bw, TN, preferred_element_type=F32)
            yacc[rows, :] += y
        return carry

    lax.fori_loop(0, nc, body, 0, unroll=4 if nc % 4 == 0 else 2)
    hf_ref[0, 0, 0] = hst[0]
    hf_ref[0, 1, 0] = hst[1]

    def epi(ri, carry):
        rows = pl.ds(pl.multiple_of(ri * 256, 256), 256)
        y = yacc[rows, :] + dpar_ref[0, h] * x_ref[rows, :].astype(F32)
        y_ref[rows, :] = (y * _silu(z_ref[rows, :].astype(F32))).astype(BF)
        return carry

    lax.fori_loop(0, T // 256, epi, 0)


def _ssd(xbc, z, dt_bias, a_log, dpar, h0, *, nseq, T, row0):
    smem = pl.BlockSpec(memory_space=pltpu.SMEM)
    return pl.pallas_call(
        functools.partial(_ssd_kernel, T=T),
        grid=(nseq, 8),
        in_specs=[smem, smem, smem,
                  pl.BlockSpec((T, 128), lambda s, h: (s, h)),
                  pl.BlockSpec((T, 128), lambda s, h: (s, 8 + h // 4)),
                  pl.BlockSpec((T, 128), lambda s, h: (s, 10 + h // 4)),
                  pl.BlockSpec((T, 128), lambda s, h: (row0 + s, Z_BZ // 128 + h)),
                  pl.BlockSpec((T, 128), lambda s, h: (row0 + s, Z_SM // 128)),
                  pl.BlockSpec((1, 2, 1, 128, 128), lambda s, h: (s, 0, h, 0, 0))],
        out_specs=[pl.BlockSpec((T, 128), lambda s, h: (s, h)),
                   pl.BlockSpec((1, 2, 1, 128, 128), lambda s, h: (s, 0, h, 0, 0))],
        out_shape=[jax.ShapeDtypeStruct((nseq * T, 1024), BF),
                   jax.ShapeDtypeStruct((nseq, 2, 8, 128, 128), F32)],
        scratch_shapes=[pltpu.VMEM((T, 128), F32), pltpu.VMEM((2, T, 128), F32), pltpu.VMEM((2, T, 128), F32),
                        pltpu.VMEM((2, 128, 128), F32)],
        compiler_params=_cp("arbitrary", "arbitrary"),
        name=f"ssd_scan_T{T}",
    )(dt_bias, a_log, dpar, xbc, xbc, xbc, z, z, h0)


def _rms_rows_kernel(x_ref, g_ref, o_ref):
    x = x_ref[...].astype(F32)
    o_ref[...] = (x * lax.rsqrt(jnp.mean(x * x, axis=-1, keepdims=True) + EPS) * g_ref[...]).astype(BF)


def _rms_rows(x, g):
    tm = 512
    n, w = x.shape
    return pl.pallas_call(
        _rms_rows_kernel,
        grid=(n // tm,),
        in_specs=[pl.BlockSpec((tm, w), lambda i: (i, 0)), pl.BlockSpec((1, w), lambda i: (0, 0))],
        out_specs=pl.BlockSpec((tm, w), lambda i: (i, 0)),
        out_shape=jax.ShapeDtypeStruct((n, w), BF),
        compiler_params=_cp("arbitrary"),
        name="ssd_out_norm",
    )(x, g.reshape(1, w))


def _rope_tables():
    t = np.arange(T_LAT)
    row = (t // GRID_W).astype(np.float32)
    col = (t % GRID_W).astype(np.float32)
    freqs = (ROPE_THETA ** (-np.arange(16, dtype=np.float32) / 16)).astype(np.float32)
    ang_r = row[:, None] * freqs[None, :]
    ang_c = col[:, None] * freqs[None, :]
    ang = np.concatenate([ang_r, ang_r, ang_c, ang_c], axis=-1).astype(np.float32)
    ang = np.concatenate([ang, ang], axis=-1)
    return jnp.cos(jnp.asarray(ang)), jnp.sin(jnp.asarray(ang))


def _rope(x, cos, sin):
    lane = lax.broadcasted_iota(jnp.int32, x.shape, 1)
    even_quarter = ((lane // 16) % 2) == 0
    partner = jnp.where(even_quarter, -pltpu.roll(x, 112, 1), pltpu.roll(x, 16, 1))
    return x * cos + partner * sin


def _mla_kv_kernel(src_ref, sm_ref, cos_ref, sin_ref, wkvb_ref, gkva_ref, gkn_ref, gkp_ref,
                   kh_ref, vh_ref, ckv_ref, *, norm_kv, rope):
    cc = src_ref[...].astype(F32)
    if norm_kv:
        cc = cc * lax.rsqrt(jnp.mean(cc * cc, axis=-1, keepdims=True) + EPS) * gkva_ref[...]
    ckv_ref[...] = cc
    kv = jnp.dot(cc.astype(BF), wkvb_ref[...], preferred_element_type=F32)
    sm = sm_ref[...].astype(F32)
    lane = lax.broadcasted_iota(jnp.int32, sm.shape, 1)
    pe = jnp.where(lane < 64, pltpu.roll(sm, 64, 1), 0.0)
    pe2 = jnp.sum(pe * pe, axis=-1, keepdims=True)
    for h in range(8):
        kn = kv[:, h * 256:h * 256 + 128]
        ri = lax.rsqrt((jnp.sum(kn * kn, axis=-1, keepdims=True) + pe2) * (1.0 / 192.0) + EPS)
        kh_ref[:, h * 256:h * 256 + 128] = (kn * ri * gkn_ref[...]).astype(BF)
        p = pe * ri * gkp_ref[...]
        if rope:
            p = _rope(p, cos_ref[...], sin_ref[...])
        kh_ref[:, h * 256 + 128:(h + 1) * 256] = p.astype(BF)
        vh_ref[:, h * 128:(h + 1) * 128] = kv[:, h * 256 + 128:(h + 1) * 256].astype(BF)


def _mla_kv(src, src_blk, sm, sm_blk, cos, sin, wkvb, gkva, gkn, gkp, *, rows, norm_kv, rope, name):
    tm = 256
    ntab = T_LAT // tm
    return pl.pallas_call(
        functools.partial(_mla_kv_kernel, norm_kv=norm_kv, rope=rope),
        grid=(rows // tm,),
        in_specs=[pl.BlockSpec((tm, 256), lambda i: (src_blk[0] + i, src_blk[1])),
                  pl.BlockSpec((tm, 128), lambda i: (sm_blk[0] + i, sm_blk[1])),
                  pl.BlockSpec((tm, 128), lambda i: (i % ntab, 0)),
                  pl.BlockSpec((tm, 128), lambda i: (i % ntab, 0)),
                  pl.BlockSpec((256, 2048), lambda i: (0, 0)),
                  pl.BlockSpec((1, 256), lambda i: (0, 0)),
                  pl.BlockSpec((1, 128), lambda i: (0, 0)),
                  pl.BlockSpec((1, 128), lambda i: (0, 0))],
        out_specs=[pl.BlockSpec((tm, 2048), lambda i: (i, 0)),
                   pl.BlockSpec((tm, 1024), lambda i: (i, 0)),
                   pl.BlockSpec((tm, 256), lambda i: (i, 0))],
        out_shape=[jax.ShapeDtypeStruct((rows, 2048), BF),
                   jax.ShapeDtypeStruct((rows, 1024), BF),
                   jax.ShapeDtypeStruct((rows, 256), F32)],
        compiler_params=_cp("arbitrary"),
        name=name,
    )(src, sm, cos, sin, wkvb, gkva, gkn, gkp)


def _mla_q_kernel(qa_ref, cos_ref, sin_ref, wqb_ref, gqa_ref, gq_ref, qh_ref, *, rope):
    qa = qa_ref[...].astype(F32)
    qa = qa * lax.rsqrt(jnp.mean(qa * qa, axis=-1, keepdims=True) + EPS) * gqa_ref[...]
    q = jnp.dot(qa.astype(BF), wqb_ref[...], preferred_element_type=F32)
    scale = 192.0 ** -0.5 * LOG2E
    for h in range(8):
        qn = q[:, h * 256:h * 256 + 128]
        qp = q[:, h * 256 + 128:(h + 1) * 256]
        ss = jnp.sum(qn * qn, axis=-1, keepdims=True) + jnp.sum(qp * qp, axis=-1, keepdims=True)
        ri = lax.rsqrt(ss * (1.0 / 192.0) + EPS)
        qh_ref[:, h * 256:h * 256 + 128] = (qn * ri * gq_ref[:, 0:128] * scale).astype(BF)
        p = qp * ri * gq_ref[:, 128:256]
        if rope:
            p = _rope(p, cos_ref[...], sin_ref[...])
        qh_ref[:, h * 256 + 128:(h + 1) * 256] = (p * scale).astype(BF)


def _mla_q(z, row_blk0, cos, sin, wqb, gqa, gq, *, rows, rope, name):
    tm = 256
    ntab = T_LAT // tm
    return pl.pallas_call(
        functools.partial(_mla_q_kernel, rope=rope),
        grid=(rows // tm,),
        in_specs=[pl.BlockSpec((tm, 768), lambda i: (row_blk0 + i, Z_QA // 768)),
                  pl.BlockSpec((tm, 128), lambda i: (i % ntab, 0)),
                  pl.BlockSpec((tm, 128), lambda i: (i % ntab, 0)),
                  pl.BlockSpec((768, 2048), lambda i: (0, 0)),
                  pl.BlockSpec((1, 768), lambda i: (0, 0)),
                  pl.BlockSpec((1, 256), lambda i: (0, 0))],
        out_specs=pl.BlockSpec((tm, 2048), lambda i: (i, 0)),
        out_shape=jax.ShapeDtypeStruct((rows, 2048), BF),
        compiler_params=_cp("arbitrary"),
        name=name,
    )(z, cos, sin, wqb, gqa, gq)


def _mla_attn_kernel(q_ref, k_ref, v_ref, o_ref):
    s = lax.dot_general(q_ref[...], k_ref[...], NT, preferred_element_type=F32)
    p = jnp.exp2(s - jnp.max(s, axis=-1, keepdims=True))
    l = jnp.sum(p, axis=-1, keepdims=True)
    o = jnp.dot(p.astype(BF), v_ref[...], preferred_element_type=F32)
    o_ref[...] = (o / l).astype(BF)


def _mla_attn(q, k, v, *, nb, tq_total, tk, name):
    tq = 256
    nq = tq_total // tq
    return pl.pallas_call(
        _mla_attn_kernel,
        grid=(nb, 8, nq),
        in_specs=[pl.BlockSpec((tq, 256), lambda b, h, i: (b * nq + i, h)),
                  pl.BlockSpec((tk, 256), lambda b, h, i: (b, h)),
                  pl.BlockSpec((tk, 128), lambda b, h, i: (b, h))],
        out_specs=pl.BlockSpec((tq, 128), lambda b, h, i: (b * nq + i, h)),
        out_shape=jax.ShapeDtypeStruct((nb * tq_total, 1024), BF),
        compiler_params=_cp("arbitrary", "arbitrary", "arbitrary"),
        name=name,
    )(q, k, v)


def _group64_norm(x, g):
    r = lax.broadcasted_iota(jnp.int32, (128, 128), 0)
    c = lax.broadcasted_iota(jnp.int32, (128, 128), 1)
    gm = jnp.where((r // 64) == (c // 64), 1.0, 0.0).astype(BF)
    outs = []
    for j in range(8):
        xj = x[:, j * 128:(j + 1) * 128]
        sq = xj * xj
        hi = sq.astype(BF)
        lo = (sq - hi.astype(F32)).astype(BF)
        ms = (jnp.dot(hi, gm, preferred_element_type=F32) + jnp.dot(lo, gm, preferred_element_type=F32)) * (1.0 / 64.0)
        outs.append(xj * lax.rsqrt(ms + EPS) * g)
    return outs


def _diff_prep_kernel(q_ref, k_ref, cos_ref, sin_ref, gq_ref, gk_ref, qh_ref, kh_ref, kc_ref, *, rope):
    qs = _group64_norm(q_ref[...].astype(F32), gq_ref[...])
    ks = _group64_norm(k_ref[...].astype(F32), gk_ref[...])
    scale = 64.0 ** -0.5 * LOG2E
    for j in range(8):
        cols = slice(j * 128, (j + 1) * 128)
        qj, kj = qs[j], ks[j]
        kc_ref[:, cols] = kj
        if rope:
            qj = _rope(qj, cos_ref[...], sin_ref[...])
            kj = _rope(kj, cos_ref[...], sin_ref[...])
        qh_ref[:, cols] = (qj * scale).astype(BF)
        kh_ref[:, cols] = kj.astype(BF)


def _diff_prep(z, row_blk0, cos, sin, gq, gk, *, rows, rope, name):
    tm = 256
    ntab = T_LAT // tm
    return pl.pallas_call(
        functools.partial(_diff_prep_kernel, rope=rope),
        grid=(rows // tm,),
        in_specs=[pl.BlockSpec((tm, 1024), lambda i: (row_blk0 + i, Z_DQ // 1024)),
                  pl.BlockSpec((tm, 1024), lambda i: (row_blk0 + i, Z_DK // 1024)),
                  pl.BlockSpec((tm, 128), lambda i: (i % ntab, 0)),
                  pl.BlockSpec((tm, 128), lambda i: (i % ntab, 0)),
                  pl.BlockSpec((1, 128), lambda i: (0, 0)),
                  pl.BlockSpec((1, 128), lambda i: (0, 0))],
        out_specs=[pl.BlockSpec((tm, 1024), lambda i: (i, 0)),
                   pl.BlockSpec((tm, 1024), lambda i: (i, 0)),
                   pl.BlockSpec((tm, 1024), lambda i: (i, 0))],
        out_shape=[jax.ShapeDtypeStruct((rows, 1024), BF),
                   jax.ShapeDtypeStruct((rows, 1024), BF),
                   jax.ShapeDtypeStruct((rows, 1024), F32)],
        compiler_params=_cp("arbitrary"),
        name=name,
    )(z, z, cos, sin, gq, gk)


def _diff_attn_kernel(q_ref, k_ref, v_ref, lam_ref, g_ref, o_ref, *, lam_init):
    q = q_ref[...]
    k = k_ref[...]
    v = v_ref[...]
    lane = lax.broadcasted_iota(jnp.int32, q.shape, 1)
    zero = jnp.zeros_like(q)
    outs = []
    for m in range(2):
        qm = jnp.where((lane < 64) if m == 0 else (lane >= 64), q, zero)
        s = lax.dot_general(qm, k, NT, preferred_element_type=F32)
        p = jnp.exp2(s - jnp.max(s, axis=-1, keepdims=True))
        l = jnp.sum(p, axis=-1, keepdims=True)
        outs.append(jnp.dot(p.astype(BF), v, preferred_element_type=F32) / l)
    lam = lam_ref[...]
    lam_full = (jnp.exp(jnp.sum(lam[0:1, :] * lam[1:2, :], axis=-1, keepdims=True))
                - jnp.exp(jnp.sum(lam[2:3, :] * lam[3:4, :], axis=-1, keepdims=True)) + lam_init)
    od = outs[0] - lam_full * outs[1]
    y = od * lax.rsqrt(jnp.mean(od * od, axis=-1, keepdims=True) + EPS) * g_ref[...]
    o_ref[...] = (y * (1.0 - lam_init)).astype(BF)


def _diff_attn(q, k, v, v_col0, lam, g, *, nb, tq_total, tk, lam_init, name):
    tq = 256
    nq = tq_total // tq
    return pl.pallas_call(
        functools.partial(_diff_attn_kernel, lam_init=lam_init),
        grid=(nb, 8, nq),
        in_specs=[pl.BlockSpec((tq, 128), lambda b, h, i: (b * nq + i, h)),
                  pl.BlockSpec((tk, 128), lambda b, h, i: (b, h)),
                  pl.BlockSpec((tk, 128), lambda b, h, i: (b, v_col0 // 128 + h)),
                  pl.BlockSpec((4, 64), lambda b, h, i: (0, 0)),
                  pl.BlockSpec((1, 128), lambda b, h, i: (0, 0))],
        out_specs=pl.BlockSpec((tq, 128), lambda b, h, i: (b * nq + i, h)),
        out_shape=jax.ShapeDtypeStruct((nb * tq_total, 1024), BF),
        compiler_params=_cp("arbitrary", "arbitrary", "arbitrary"),
        name=name,
    )(q, k, v, lam, g)


PEER_TOPK = 16
_CAND_GROUPS = [(0, 16), (1, 8), (2, 5), (3, 4), (4, 3), (5, 2), (6, 2), (7, 2)]


def _top16_rows(s, payload=None):
    nrow = s.shape[0]
    row = lax.broadcasted_iota(jnp.int32, s.shape, 0)
    vals, idxs, pays = [], [], [[] for _ in (payload or [])]
    for _ in range(PEER_TOPK):
        m = jnp.max(s, axis=0, keepdims=True)
        i = jnp.min(jnp.where(s == m, row, nrow), axis=0, keepdims=True)
        hit = row == i
        vals.append(m)
        idxs.append(i)
        for a, pay in enumerate(payload or []):
            pays[a].append(jnp.max(jnp.where(hit, pay, -1), axis=0, keepdims=True))
        s = jnp.where(hit, -jnp.inf, s)
    return (jnp.concatenate(vals, axis=0), jnp.concatenate(idxs, axis=0),
            [jnp.concatenate(p, axis=0) for p in pays])


def _peer_route_kernel(q_ref, keys_ref, w_ref, a_s, b_s, g_s, *, tn):
    sub = lax.broadcasted_iota(jnp.int32, (8, tn), 0)
    for h in range(8):
        tops = []
        for p in range(2):
            qhp = q_ref[:, (h * 2 + p) * 128:(h * 2 + p + 1) * 128]
            st = lax.dot_general(keys_ref[h, p], qhp, NT, preferred_element_type=F32)
            v, i, _ = _top16_rows(st)
            tops.append((v, i))
        (s1, i1), (s2, i2) = tops
        cs, ca, cb = [], [], []
        for r, nvalid in _CAND_GROUPS:
            width = 16 if r == 0 else 8
            blk = s1[r:r + 1, :] + s2[0:width, :]
            if nvalid < width:
                blk = jnp.where(sub < nvalid, blk, -jnp.inf)
            cs.append(blk)
            ca.append(jnp.broadcast_to(i1[r:r + 1, :], (width, tn)))
            cb.append(i2[0:width, :])
        cs.append(s1[8:16, :] + s2[0:1, :])
        ca.append(i1[8:16, :])
        cb.append(jnp.broadcast_to(i2[0:1, :], (8, tn)))
        cand_s = jnp.concatenate(cs, axis=0)
        cand_a = jnp.concatenate(ca, axis=0)
        cand_b = jnp.concatenate(cb, axis=0)
        top_s, _, (sel_a, sel_b) = _top16_rows(cand_s, [cand_a, cand_b])
        e = jnp.exp(top_s - top_s[0:1, :])
        g = e / jnp.sum(e, axis=0, keepdims=True)
        a_s[:, h * 16:(h + 1) * 16] = sel_a.astype(F32).T
        b_s[:, h * 16:(h + 1) * 16] = sel_b.astype(F32).T
        g_s[:, h * 16:(h + 1) * 16] = g.T

    ids = lax.broadcasted_iota(jnp.int32, (128, 128), 0).astype(F32)

    def build(t, carry):
        n0 = pl.multiple_of(t * 16, 16)
        a_rows = a_s[pl.ds(n0, 16), :]
        b_rows = b_s[pl.ds(n0, 16), :]
        g_rows = g_s[pl.ds(n0, 16), :]
        ws = []
        for r in range(16):
            pt = jnp.where(ids == a_rows[r:r + 1, :], g_rows[r:r + 1, :], 0.0).astype(BF)
            qt = jnp.where(ids == b_rows[r:r + 1, :], 1.0, 0.0).astype(BF)
            ws.append(lax.dot_general(pt, qt, NT, preferred_element_type=F32))
        w_ref[:, pl.ds(n0, 16), :] = jnp.swapaxes(jnp.stack(ws, axis=0), 0, 1).astype(BF)
        return carry

    lax.fori_loop(0, tn // 16, build, 0)


def _peer_route(q, keys, layer):
    tn = 128
    n = q.shape[0]
    return pl.pallas_call(
        functools.partial(_peer_route_kernel, tn=tn),
        grid=(n // tn,),
        in_specs=[pl.BlockSpec((tn, 2048), lambda i: (i, 0)),
                  pl.BlockSpec((None, 8, 2, 128, 128), lambda i: (layer, 0, 0, 0, 0))],
        out_specs=pl.BlockSpec((128, tn, 128), lambda i: (0, i, 0)),
        out_shape=jax.ShapeDtypeStruct((128, n, 128), BF),
        scratch_shapes=[pltpu.VMEM((tn, 128), F32), pltpu.VMEM((tn, 128), F32), pltpu.VMEM((tn, 128), F32)],
        compiler_params=_cp("arbitrary"),
        name="peer_route",
    )(q, keys)


def _gelu_tanh(x):
    return 0.5 * x * (1.0 + jnp.tanh(0.7978845608028654 * (x + 0.044715 * x * x * x)))


def _peer_dense_kernel(x_ref, u_ref, v_ref, w_ref, h_ref, mod_ref, o_ref):
    @pl.when(pl.program_id(1) == 0)
    def _():
        o_ref[...] = jnp.zeros_like(o_ref)

    hid = lax.dot_general(x_ref[...], u_ref[...], NT, preferred_element_type=F32)
    w = jnp.concatenate([w_ref[a] for a in range(w_ref.shape[0])], axis=1)
    act = (_gelu_tanh(hid) * w.astype(F32)).astype(BF)
    o_ref[...] += jnp.dot(act, v_ref[...], preferred_element_type=F32)

    @pl.when(pl.program_id(1) == pl.num_programs(1) - 1)
    def _():
        o_ref[...] = h_ref[...] + mod_ref[0][5:6, :] * o_ref[...]


def _peer_dense(x, u, v, w, layer, h, mod3):
    tn, te = 512, 512
    n = x.shape[0]
    ne = u.shape[1]
    return pl.pallas_call(
        _peer_dense_kernel,
        grid=(n // tn, ne // te),
        in_specs=[pl.BlockSpec((tn, D), lambda i, e: (i, 0), pipeline_mode=pl.Buffered(1)),
                  pl.BlockSpec((None, te, D), lambda i, e: (layer, e, 0)),
                  pl.BlockSpec((None, te, D), lambda i, e: (layer, e, 0)),
                  pl.BlockSpec((te // 128, tn, 128), lambda i, e: (e, i, 0)),
                  pl.BlockSpec((tn, D), lambda i, e: (i, 0), pipeline_mode=pl.Buffered(1)),
                  pl.BlockSpec((1, 6, D), lambda i, e: (i * tn // MOD_ROWS, 0, 0))],
        out_specs=pl.BlockSpec((tn, D), lambda i, e: (i, 0), pipeline_mode=pl.Buffered(1)),
        out_shape=jax.ShapeDtypeStruct((n, D), F32),
        compiler_params=_cp("arbitrary", "arbitrary"),
        name="peer_dense",
    )(x, u, v, w, h, mod3)


def _permute_w_in(w):
    o = dict(a_q=0, a_k=512, a_v=1024, a_g=2048, a_gk=3072, b_z=3104, b_xbc=4128, b_dt=5664,
             c_qa=5680, c_kva=6448, c_kpe=6704, d_q=6768, d_k=7792, d_v=8816)

    def seg(name, width):
        return w[..., o[name]:o[name] + width]

    zeros = lambda n: jnp.zeros(w.shape[:-1] + (n,), w.dtype)
    parts = [seg('a_v', 1024), seg('a_g', 1024), seg('b_z', 1024), seg('d_q', 1024), seg('d_k', 1024),
             seg('d_v', 1024), seg('b_xbc', 1536), seg('a_q', 512), seg('a_k', 512), seg('c_kva', 256),
             seg('a_gk', 32), seg('b_dt', 16), zeros(16), seg('c_kpe', 64), zeros(128), seg('c_qa', 768)]
    return jnp.concatenate(parts, axis=-1).astype(BF)


def _permute_w_qb(w):
    w3 = w.reshape(w.shape[0], 8, 192)
    return jnp.concatenate([w3, jnp.zeros((w.shape[0], 8, 64), w.dtype)], axis=-1).reshape(w.shape[0], 2048).astype(BF)


def kernel(x_prompt, x_sample, cache_mla_ckv, cache_mla_kpe, cache_diff_k, cache_diff_v, state_gla, state_ssd, c, c_ctx, norm1_g, norm2_g, w_ada, b_ada, w_in, w_out, gla_w_gk, gla_b_gk, gla_norm_g, ssd_conv_w, ssd_conv_b, ssd_a_log, ssd_dt_bias, ssd_d, ssd_norm_g, mla_qa_norm_g, mla_w_qb, mla_kva_norm_g, mla_w_kvb, mla_q_norm_g, mla_k_norm_g, diff_q_norm_g, diff_k_norm_g, diff_lambda, diff_subln_g, peer_w_q, peer_sub_keys, peer_u, peer_v):
    depth = w_in.shape[0]
    w_in_b = _permute_w_in(w_in)
    w_out_b = w_out.astype(BF)
    peer_wq_b = peer_w_q.astype(BF)
    peer_keys_b = peer_sub_keys.astype(BF)
    peer_u_b = peer_u.astype(BF)
    peer_v_b = peer_v.astype(BF)
    x = jnp.concatenate([x_prompt.reshape(N_CTX, D), x_sample.reshape(N_LAT, D)], axis=0)
    cv8 = jnp.concatenate([c_ctx[None, :], c, jnp.zeros((5, D), F32)], axis=0)
    cos_t, sin_t = _rope_tables()
    nct = N_CTX // T_CTX
    outs = {k: [] for k in ('ckv', 'kpe', 'dk', 'dv', 'gla', 'ssd')}

    for l in range(depth):
        mod3 = _modulation(cv8, w_ada, b_ada, l).reshape(8, 6, D)[:3]

        (z,) = _norm_mm(x, norm1_g[l], mod3, w_in_b, l, shift_row=0, tn=768, emit_xn=False, name="norm_in_proj")

        wgk = gla_w_gk[l]
        wgk_pad = jnp.zeros((2, 128, 512), F32)
        wgk_pad = wgk_pad.at[0, SM_GK:SM_GK + 16].set(wgk[0]).at[1, SM_GK + 16:SM_GK + 32].set(wgk[1]).astype(BF)
        bgk = gla_b_gk[l].reshape(2, 1, 512)
        ng = gla_norm_g[l].reshape(1, 128)
        a_ctx, gla_fin = _gla(z, wgk_pad, bgk, ng, jnp.zeros((B_CTX, 2, 8, 64, 128), F32),
                              nseq=B_CTX, T=T_CTX, row0=0)
        a_lat, _ = _gla(z, wgk_pad, bgk, ng, state_gla[:, l], nseq=B_LAT, T=T_LAT, row0=N_CTX // T_LAT)
        out_a = jnp.concatenate([a_ctx, a_lat], axis=0)

        conv_b = ssd_conv_b[l].reshape(1, 1536)
        xbc_ctx = _conv(z, ssd_conv_w[l], conv_b, nseq=B_CTX, T=T_CTX, row0=0)
        xbc_lat = _conv(z, ssd_conv_w[l], conv_b, nseq=B_LAT, T=T_LAT, row0=N_CTX // T_LAT)
        dpar = ssd_d[l].reshape(1, 8)
        y_ctx, ssd_fin = _ssd(xbc_ctx, z, ssd_dt_bias[l], ssd_a_log[l], dpar,
                              jnp.zeros((B_CTX, 2, 8, 128, 128), F32), nseq=B_CTX, T=T_CTX, row0=0)
        y_lat, _ = _ssd(xbc_lat, z, ssd_dt_bias[l], ssd_a_log[l], dpar, state_ssd[:, l],
                        nseq=B_LAT, T=T_LAT, row0=N_CTX // T_LAT)
        out_b = _rms_rows(jnp.concatenate([y_ctx, y_lat], axis=0), ssd_norm_g[l])

        wkvb = mla_w_kvb[l].astype(BF)
        wqb = _permute_w_qb(mla_w_qb[l])
        gkva = mla_kva_norm_g[l].reshape(1, 256)
        gk = mla_k_norm_g[l]
        gkn = gk[:128].reshape(1, 128)
        gkp = jnp.concatenate([gk[128:], jnp.zeros((64,), F32)]).reshape(1, 128)
        gq = jnp.concatenate([mla_q_norm_g[l], jnp.zeros((64,), F32)]).reshape(1, 256)
        gqa = mla_qa_norm_g[l].reshape(1, 768)
        kh_ctx, vh_ctx, ckv_ctx = _mla_kv(z, (0, Z_KVA // 256), z, (0, Z_SM // 128), cos_t, sin_t, wkvb, gkva, gkn, gkp,
                                          rows=N_CTX, norm_kv=True, rope=False, name="mla_kv_ctx")
        kh_lat, vh_lat, _ = _mla_kv(z, (N_CTX // 256, Z_KVA // 256), z, (N_CTX // 256, Z_SM // 128), cos_t, sin_t,
                                    wkvb, gkva, gkn, gkp, rows=N_LAT, norm_kv=True, rope=True, name="mla_kv_lat")
        cache_sm = jnp.concatenate([jnp.zeros((B_LAT * PAST, 64), F32),
                                    cache_mla_kpe[:, l].reshape(B_LAT * PAST, 64)], axis=1)
        kh_past, vh_past, _ = _mla_kv(cache_mla_ckv[:, l].reshape(B_LAT * PAST, 256), (0, 0), cache_sm, (0, 0),
                                      cos_t, sin_t, wkvb, gkva, gkn, gkp,
                                      rows=B_LAT * PAST, norm_kv=False, rope=False, name="mla_kv_past")
        qh_ctx = _mla_q(z, 0, cos_t, sin_t, wqb, gqa, gq, rows=N_CTX, rope=False, name="mla_q_ctx")
        qh_lat = _mla_q(z, N_CTX // 256, cos_t, sin_t, wqb, gqa, gq, rows=N_LAT, rope=True, name="mla_q_lat")
        tk_lat = PAST + T_LAT
        k_lat = jnp.concatenate([kh_past.reshape(B_LAT, PAST, 2048), kh_lat.reshape(B_LAT, T_LAT, 2048)],
                                axis=1).reshape(B_LAT * tk_lat, 2048)
        v_lat = jnp.concatenate([vh_past.reshape(B_LAT, PAST, 1024), vh_lat.reshape(B_LAT, T_LAT, 1024)],
                                axis=1).reshape(B_LAT * tk_lat, 1024)
        c_ctx_o = _mla_attn(qh_ctx, kh_ctx, vh_ctx, nb=B_CTX, tq_total=T_CTX, tk=T_CTX, name="mla_attn_ctx")
        c_lat_o = _mla_attn(qh_lat, k_lat, v_lat, nb=B_LAT, tq_total=T_LAT, tk=tk_lat, name="mla_attn_lat")
        out_c = jnp.concatenate([c_ctx_o, c_lat_o], axis=0)

        gdq = jnp.tile(diff_q_norm_g[l], 2).reshape(1, 128)
        gdk = jnp.tile(diff_k_norm_g[l], 2).reshape(1, 128)
        dq_ctx, dk_ctx, dkc_ctx = _diff_prep(z, 0, cos_t, sin_t, gdq, gdk, rows=N_CTX, rope=False, name="diff_prep_ctx")
        dq_lat, dk_lat, _ = _diff_prep(z, N_CTX // 256, cos_t, sin_t, gdq, gdk, rows=N_LAT, rope=True,
                                       name="diff_prep_lat")
        dk_full = jnp.concatenate([cache_diff_k[:, l].reshape(B_LAT, PAST, 1024).astype(BF),
                                   dk_lat.reshape(B_LAT, T_LAT, 1024)], axis=1).reshape(B_LAT * tk_lat, 1024)
        dv_full = jnp.concatenate([cache_diff_v[:, l].reshape(B_LAT, PAST, 1024).astype(BF),
                                   z[N_CTX:, Z_DV:Z_DV + 1024].reshape(B_LAT, T_LAT, 1024)],
                                  axis=1).reshape(B_LAT * tk_lat, 1024)
        lam_init = 0.8 - 0.6 * math.exp(-0.3 * l)
        gsub = diff_subln_g[l].reshape(1, 128)
        d_ctx_o = _diff_attn(dq_ctx, dk_ctx, z, Z_DV, diff_lambda[l], gsub, nb=B_CTX, tq_total=T_CTX, tk=T_CTX,
                             lam_init=lam_init, name="diff_attn_ctx")
        d_lat_o = _diff_attn(dq_lat, dk_full, dv_full, 0, diff_lambda[l], gsub, nb=B_LAT, tq_total=T_LAT, tk=tk_lat,
                             lam_init=lam_init, name="diff_attn_lat")
        out_d = jnp.concatenate([d_ctx_o, d_lat_o], axis=0)

        hmid = _out_proj((out_a, out_b, out_c, out_d), w_out_b, l, x, mod3)

        pq, u2 = _norm_mm(hmid, norm2_g[l], mod3, peer_wq_b, l, shift_row=3, tn=512, emit_xn=True,
                          name="norm_peer_query")
        wmap = _peer_route(pq, peer_keys_b, l)
        x = _peer_dense(u2, peer_u_b, peer_v_b, wmap, l, hmid, mod3)

        outs['ckv'].append(ckv_ctx.reshape(B_CTX, T_CTX, 256))
        outs['kpe'].append(z[:N_CTX, Z_SM + SM_KPE:Z_SM + 128].astype(F32).reshape(B_CTX, T_CTX, 64))
        outs['dk'].append(dkc_ctx.reshape(B_CTX, T_CTX, 8, 2, 64))
        outs['dv'].append(z[:N_CTX, Z_DV:Z_DV + 1024].astype(F32).reshape(B_CTX, T_CTX, 8, 128))
        outs['gla'].append(gla_fin)
        outs['ssd'].append(ssd_fin)

    y_p = x[:N_CTX].reshape(B_CTX, T_CTX, D)
    y_s = x[N_CTX:].reshape(B_LAT, T_LAT, D)
    return (y_p, y_s, jnp.stack(outs['ckv'], axis=1), jnp.stack(outs['kpe'], axis=1),
            jnp.stack(outs['dk'], axis=1), jnp.stack(outs['dv'], axis=1),
            jnp.stack(outs['gla'], axis=1), jnp.stack(outs['ssd'], axis=1))
```

```python
import functools
import math

import jax
import jax.numpy as jnp
import numpy as np
from jax import lax
from jax.experimental import pallas as pl
from jax.experimental.pallas import tpu as pltpu

F32 = jnp.float32
BF = jnp.bfloat16
LOG2E = 1.4426950408889634

D = 4096
EPS = 1e-6
N_CTX = 4096
N_LAT = 8192
N_TOK = N_CTX + N_LAT
MOD_ROWS = 4096
T_CTX, B_CTX = 256, 16
T_LAT, B_LAT = 4096, 2
PAST = 512
GRID_W = 64
ROPE_THETA = 10000.0

V7X_VMEM_LIMIT = 56 * 1024 * 1024

Z_AV, Z_AG, Z_BZ, Z_DQ, Z_DK, Z_DV = 0, 1024, 2048, 3072, 4096, 5120
Z_XBC, Z_AQ, Z_AK, Z_KVA, Z_SM, Z_QA = 6144, 7680, 8192, 8704, 8960, 9216
Z_W = 9984
SM_GK, SM_DT, SM_KPE = 0, 32, 64

NT = (((1,), (1,)), ((), ()))
TN = (((0,), (0,)), ((), ()))


def _cp(*sem):
    return pltpu.CompilerParams(dimension_semantics=sem, vmem_limit_bytes=V7X_VMEM_LIMIT)


def _sigmoid(x):
    return 1.0 / (1.0 + jnp.exp(-x))


def _silu(x):
    return x * _sigmoid(x)


def _softplus(x):
    return jnp.maximum(x, 0.0) + jnp.log1p(jnp.exp(-jnp.abs(x)))


def _log_sigmoid(x):
    return jnp.minimum(x, 0.0) - jnp.log1p(jnp.exp(-jnp.abs(x)))


def _block_cumsum(x, blk, reverse):
    n = x.shape[0]
    row = lax.broadcasted_iota(jnp.int32, x.shape, 0) % blk
    step = 1
    while step < blk:
        if reverse:
            x = x + jnp.where(row < blk - step, pltpu.roll(x, n - step, 0), 0.0)
        else:
            x = x + jnp.where(row >= step, pltpu.roll(x, step, 0), 0.0)
        step *= 2
    return x


def _mod_kernel(c_ref, w_ref, b_ref, o_ref):
    a = _silu(c_ref[...])
    o_ref[...] = jnp.dot(a.astype(BF), w_ref[...].astype(BF), preferred_element_type=F32) + b_ref[...]


def _modulation(cv8, w_ada, b_ada, layer):
    tn = 512
    return pl.pallas_call(
        _mod_kernel,
        grid=(6 * D // tn,),
        in_specs=[pl.BlockSpec((8, D), lambda j: (0, 0)),
                  pl.BlockSpec((None, D, tn), lambda j: (layer, 0, j)),
                  pl.BlockSpec((None, 1, tn), lambda j: (layer, 0, j))],
        out_specs=pl.BlockSpec((8, tn), lambda j: (0, j)),
        out_shape=jax.ShapeDtypeStruct((8, 6 * D), F32),
        compiler_params=_cp("arbitrary"),
        name="modulation",
    )(cv8, w_ada, b_ada.reshape(b_ada.shape[0], 1, 6 * D))


def _norm_mm_kernel(x_ref, g_ref, mod_ref, w_ref, o_ref, *rest, shift_row, emit_xn):
    if emit_xn:
        xn_out_ref, xn = rest
    else:
        (xn,) = rest

    @pl.when(pl.program_id(1) == 0)
    def _():
        m = mod_ref[0]
        gain = g_ref[...] * (1.0 + m[shift_row + 1:shift_row + 2, :])
        shift = m[shift_row:shift_row + 1, :]

        def chunk(r, carry):
            rows = pl.ds(pl.multiple_of(r * 64, 64), 64)
            x = x_ref[rows, :]
            u = (x * lax.rsqrt(jnp.mean(x * x, axis=-1, keepdims=True) + EPS) * gain + shift).astype(BF)
            xn[rows, :] = u
            if emit_xn:
                xn_out_ref[rows, :] = u
            return carry

        lax.fori_loop(0, x_ref.shape[0] // 64, chunk, 0)

    o_ref[...] = jnp.dot(xn[...], w_ref[...], preferred_element_type=F32).astype(o_ref.dtype)


def _norm_mm(x, g, mod3, w, layer, *, shift_row, tn, emit_xn, name):
    tm = 512
    n, k = x.shape
    nout = w.shape[2]
    out_shape = [jax.ShapeDtypeStruct((n, nout), BF)]
    out_specs = [pl.BlockSpec((tm, tn), lambda i, j: (i, j))]
    if emit_xn:
        out_shape.append(jax.ShapeDtypeStruct((n, k), BF))
        out_specs.append(pl.BlockSpec((tm, k), lambda i, j: (i, 0)))
    res = pl.pallas_call(
        functools.partial(_norm_mm_kernel, shift_row=shift_row, emit_xn=emit_xn),
        grid=(n // tm, nout // tn),
        in_specs=[pl.BlockSpec((tm, k), lambda i, j: (i, 0)),
                  pl.BlockSpec((1, k), lambda i, j: (0, 0)),
                  pl.BlockSpec((1, 6, k), lambda i, j: (i * tm // MOD_ROWS, 0, 0)),
                  pl.BlockSpec((None, k, tn), lambda i, j: (layer, 0, j))],
        out_specs=out_specs,
        out_shape=out_shape,
        scratch_shapes=[pltpu.VMEM((tm, k), BF)],
        compiler_params=_cp("arbitrary", "arbitrary"),
        name=name,
    )(x, g.reshape(1, k), mod3, w)
    return res


def _out_proj_kernel(a_ref, b_ref, c_ref, d_ref, w_ref, x_ref, mod_ref, o_ref):
    acc = jnp.dot(a_ref[...], w_ref[0:1024, :], preferred_element_type=F32)
    acc += jnp.dot(b_ref[...], w_ref[1024:2048, :], preferred_element_type=F32)
    acc += jnp.dot(c_ref[...], w_ref[2048:3072, :], preferred_element_type=F32)
    acc += jnp.dot(d_ref[...], w_ref[3072:4096, :], preferred_element_type=F32)
    o_ref[...] = x_ref[...] + mod_ref[0][2:3, :] * acc


def _out_proj(mix, w, layer, x, mod3):
    tm, tn = 512, 1024
    n = x.shape[0]
    mspec = pl.BlockSpec((tm, 1024), lambda i, j: (i, 0))
    return pl.pallas_call(
        _out_proj_kernel,
        grid=(n // tm, D // tn),
        in_specs=[mspec, mspec, mspec, mspec,
                  pl.BlockSpec((None, D, tn), lambda i, j: (layer, 0, j)),
                  pl.BlockSpec((tm, tn), lambda i, j: (i, j)),
                  pl.BlockSpec((1, 6, tn), lambda i, j: (i * tm // MOD_ROWS, 0, j))],
        out_specs=pl.BlockSpec((tm, tn), lambda i, j: (i, j)),
        out_shape=jax.ShapeDtypeStruct((n, D), F32),
        compiler_params=_cp("arbitrary", "arbitrary"),
        name="out_proj",
    )(*mix, w, x, mod3)


GLA_BLK = 16


def _gla_kernel(q_ref, k_ref, v_ref, g_ref, sm_ref, wgk_ref, bgk_ref, ng_ref, s0_ref,
                o_ref, sf_ref, oacc, cumf, cumb, st, *, T):
    nb = T // GLA_BLK

    def gate_body(r, carry):
        rows = pl.ds(pl.multiple_of(r * 256, 256), 256)
        smb = sm_ref[rows, :]
        gf = jnp.dot(smb, wgk_ref[0], preferred_element_type=F32) + bgk_ref[0]
        gb = jnp.dot(smb, wgk_ref[1], preferred_element_type=F32) + bgk_ref[1]
        laf = _log_sigmoid(gf) * (1.0 / 16.0)
        lab = _log_sigmoid(gb) * (1.0 / 16.0)
        cumf[rows, :] = _block_cumsum(laf, GLA_BLK, reverse=False)
        cumb[rows, :] = _block_cumsum(lab, GLA_BLK, reverse=True)
        oacc[rows, :] = jnp.zeros((256, 256), F32)
        return carry

    lax.fori_loop(0, T // 256, gate_body, 0)

    for d in range(2):
        for h in range(2):
            st[d, h] = s0_ref[0, d, h].T

    scale = 64.0 ** -0.5
    rowi = lax.broadcasted_iota(jnp.int32, (GLA_BLK, 128), 0)
    sel_r = lax.broadcasted_iota(jnp.int32, (128, 256), 0)
    sel_c = lax.broadcasted_iota(jnp.int32, (128, 256), 1)
    head_sel = jnp.where((sel_r // 64) == (sel_c // 128), 1.0, 0.0).astype(BF)

    def body(i, carry):
        for d in range(2):
            blk = i if d == 0 else nb - 1 - i
            rows = pl.ds(pl.multiple_of(blk * GLA_BLK, GLA_BLK), GLA_BLK)
            qb = q_ref[rows, :].astype(F32) * scale
            kb = k_ref[rows, :].astype(F32)
            vb = v_ref[rows, :]
            vf = vb.astype(F32)
            cb = cumf[rows, :] if d == 0 else cumb[rows, :]
            edge = cb[GLA_BLK - 1:GLA_BLK, :] if d == 0 else cb[0:1, :]
            qh = qb * jnp.exp(cb)
            kt = kb * jnp.exp(edge - cb)
            dec = jnp.exp(edge)
            ws = []
            for s in range(GLA_BLK):
                msk = (rowi >= s) if d == 0 else (rowi <= s)
                w = qb * kb[s:s + 1, :] * jnp.exp(jnp.minimum(cb - cb[s:s + 1, :], 0.0))
                ws.append(jnp.where(msk, w, 0.0))
            wst = jnp.concatenate(ws, axis=0).astype(BF)
            att = jnp.dot(wst, head_sel, preferred_element_type=F32)
            od = jnp.zeros((GLA_BLK, 256), F32)
            for s in range(GLA_BLK):
                od = od + att[s * GLA_BLK:(s + 1) * GLA_BLK, :] * vf[s:s + 1, :]
            od0 = od[:, 0:128]
            od1 = od[:, 128:256]
            for h in range(2):
                hs = slice(h * 64, h * 64 + 64)
                s_t = st[d, h]
                o_h = lax.dot_general(qh[:, hs].astype(BF), s_t.astype(BF), NT, preferred_element_type=F32)
                o_h = o_h + (od0 if h == 0 else od1)
                upd = lax.dot_general(vb[:, h * 128:(h + 1) * 128], kt[:, hs].astype(BF), TN,
                                      preferred_element_type=F32)
                st[d, h] = s_t * dec[:, hs] + upd
                oacc[rows, h * 128:(h + 1) * 128] += o_h
        return carry

    lax.fori_loop(0, nb, body, 0, unroll=2)

    for d in range(2):
        for h in range(2):
            sf_ref[0, d, h] = st[d, h].T

    def epi(r, carry):
        rows = pl.ds(pl.multiple_of(r * 256, 256), 256)
        for h in range(2):
            cols = slice(h * 128, (h + 1) * 128)
            o = oacc[rows, cols]
            y = o * lax.rsqrt(jnp.mean(o * o, axis=-1, keepdims=True) + EPS) * ng_ref[...]
            o_ref[rows, cols] = (y * _silu(g_ref[rows, cols].astype(F32))).astype(BF)
        return carry

    lax.fori_loop(0, T // 256, epi, 0)


def _gla(z, wgk_pad, bgk, ng, s0, *, nseq, T, row0):
    return pl.pallas_call(
        functools.partial(_gla_kernel, T=T),
        grid=(nseq, 4),
        in_specs=[pl.BlockSpec((T, 128), lambda s, p: (row0 + s, Z_AQ // 128 + p)),
                  pl.BlockSpec((T, 128), lambda s, p: (row0 + s, Z_AK // 128 + p)),
                  pl.BlockSpec((T, 256), lambda s, p: (row0 + s, Z_AV // 256 + p)),
                  pl.BlockSpec((T, 256), lambda s, p: (row0 + s, Z_AG // 256 + p)),
                  pl.BlockSpec((T, 128), lambda s, p: (row0 + s, Z_SM // 128)),
                  pl.BlockSpec((2, 128, 128), lambda s, p: (0, 0, p)),
                  pl.BlockSpec((2, 1, 128), lambda s, p: (0, 0, p)),
                  pl.BlockSpec((1, 128), lambda s, p: (0, 0)),
                  pl.BlockSpec((1, 2, 2, 64, 128), lambda s, p: (s, 0, p, 0, 0))],
        out_specs=[pl.BlockSpec((T, 256), lambda s, p: (s, p)),
                   pl.BlockSpec((1, 2, 2, 64, 128), lambda s, p: (s, 0, p, 0, 0))],
        out_shape=[jax.ShapeDtypeStruct((nseq * T, 1024), BF),
                   jax.ShapeDtypeStruct((nseq, 2, 8, 64, 128), F32)],
        scratch_shapes=[pltpu.VMEM((T, 256), F32), pltpu.VMEM((T, 128), F32), pltpu.VMEM((T, 128), F32),
                        pltpu.VMEM((2, 2, 128, 64), F32)],
        compiler_params=_cp("arbitrary", "arbitrary"),
        name=f"gla_T{T}",
    )(z, z, z, z, z, wgk_pad, bgk, ng, s0)


def _conv_kernel(x_ref, w_ref, b_ref, o_ref, xp, *, T):
    xp[0:8, :] = jnp.zeros((8, 256), F32)
    xp[8 + T:16 + T, :] = jnp.zeros((8, 256), F32)
    xp[8:8 + T, :] = x_ref[...].astype(F32)
    for r in range(T // 256):
        acc = b_ref[...] + w_ref[0:1, :] * xp[6 + r * 256:6 + (r + 1) * 256, :]
        for kk in range(1, 5):
            acc = acc + w_ref[kk:kk + 1, :] * xp[6 + kk + r * 256:6 + kk + (r + 1) * 256, :]
        o_ref[r * 256:(r + 1) * 256, :] = _silu(acc).astype(BF)


def _conv(z, w, b, *, nseq, T, row0):
    return pl.pallas_call(
        functools.partial(_conv_kernel, T=T),
        grid=(nseq, 6),
        in_specs=[pl.BlockSpec((T, 256), lambda s, j: (row0 + s, Z_XBC // 256 + j)),
                  pl.BlockSpec((5, 256), lambda s, j: (0, j)),
                  pl.BlockSpec((1, 256), lambda s, j: (0, j))],
        out_specs=pl.BlockSpec((T, 256), lambda s, j: (s, j)),
        out_shape=jax.ShapeDtypeStruct((nseq * T, 1536), BF),
        scratch_shapes=[pltpu.VMEM((T + 16, 256), F32)],
        compiler_params=_cp("arbitrary", "arbitrary"),
        name=f"ssd_conv_T{T}",
    )(z, w, b)


SSD_CHUNK = 128


def _ssd_kernel(bias_ref, alog_ref, dpar_ref, x_ref, b_ref, c_ref, z_ref, sm_ref, h0_ref,
                y_ref, hf_ref, yacc, dts, cums, hst, *, T):
    h = pl.program_id(1)
    nc = T // SSD_CHUNK
    r = lax.broadcasted_iota(jnp.int32, (128, 128), 0)
    c = lax.broadcasted_iota(jnp.int32, (128, 128), 1)
    hst[0] = h0_ref[0, 0, 0]
    hst[1] = h0_ref[0, 1, 0]

    def prep(ci, carry):
        rows = pl.ds(pl.multiple_of(ci * SSD_CHUNK, SSD_CHUNK), SSD_CHUNK)
        yacc[rows, :] = jnp.zeros((SSD_CHUNK, 128), F32)
        for d in range(2):
            sel = jnp.where(r == SM_DT + d * 8 + h, 1.0, 0.0).astype(BF)
            raw = jnp.dot(sm_ref[rows, :], sel, preferred_element_type=F32)
            dtb = _softplus(raw + bias_ref[d, h])
            a_neg = -jnp.exp(jnp.full((1, 128), alog_ref[d, h], F32))
            dts[d, rows, :] = dtb
            cums[d, rows, :] = _block_cumsum(dtb * a_neg, SSD_CHUNK, reverse=(d == 1))
        return carry

    lax.fori_loop(0, nc, prep, 0, unroll=4 if nc % 4 == 0 else 2)

    def body(ci, carry):
        for d in range(2):
            blk = ci if d == 0 else nc - 1 - ci
            rows = pl.ds(pl.multiple_of(blk * SSD_CHUNK, SSD_CHUNK), SSD_CHUNK)
            dtb = dts[d, rows, :]
            cb = cums[d, rows, :]
            c_t = cb.T
            dt_t = dtb.T
            msk = (r >= c) if d == 0 else (r <= c)
            seg = jnp.exp(jnp.where(msk, cb - c_t, -jnp.inf))
            cm = c_ref[rows, :]
            bm = b_ref[rows, :]
            xb = x_ref[rows, :]
            scores = lax.dot_general(cm, bm, NT, preferred_element_type=F32) * seg * dt_t
            hs = hst[d]
            y = jnp.dot(scores.astype(BF), xb, preferred_element_type=F32)
            y = y + lax.dot_general(cm, hs.astype(BF), NT, preferred_element_type=F32) * jnp.exp(cb)
            edge = cb[SSD_CHUNK - 1:SSD_CHUNK, :] if d == 0 else cb[0:1, :]
            wgt = jnp.exp(edge - cb) * dtb
            bw = (bm.astype(F32) * wgt).astype(BF)
            hst[d] = jnp.exp(edge) * hs + lax.dot_general(xb, bw, TN, preferred_element_type=F32)
            yacc[rows, :] += y
        return carry

    lax.fori_loop(0, nc, body, 0, unroll=4 if nc % 4 == 0 else 2)
    hf_ref[0, 0, 0] = hst[0]
    hf_ref[0, 1, 0] = hst[1]

    def epi(ri, carry):
        rows = pl.ds(pl.multiple_of(ri * 256, 256), 256)
        y = yacc[rows, :] + dpar_ref[0, h] * x_ref[rows, :].astype(F32)
        y_ref[rows, :] = (y * _silu(z_ref[rows, :].astype(F32))).astype(BF)
        return carry

    lax.fori_loop(0, T // 256, epi, 0)


def _ssd(xbc, z, dt_bias, a_log, dpar, h0, *, nseq, T, row0):
    smem = pl.BlockSpec(memory_space=pltpu.SMEM)
    return pl.pallas_call(
        functools.partial(_ssd_kernel, T=T),
        grid=(nseq, 8),
        in_specs=[smem, smem, smem,
                  pl.BlockSpec((T, 128), lambda s, h: (s, h)),
                  pl.BlockSpec((T, 128), lambda s, h: (s, 8 + h // 4)),
                  pl.BlockSpec((T, 128), lambda s, h: (s, 10 + h // 4)),
                  pl.BlockSpec((T, 128), lambda s, h: (row0 + s, Z_BZ // 128 + h)),
                  pl.BlockSpec((T, 128), lambda s, h: (row0 + s, Z_SM // 128)),
                  pl.BlockSpec((1, 2, 1, 128, 128), lambda s, h: (s, 0, h, 0, 0))],
        out_specs=[pl.BlockSpec((T, 128), lambda s, h: (s, h)),
                   pl.BlockSpec((1, 2, 1, 128, 128), lambda s, h: (s, 0, h, 0, 0))],
        out_shape=[jax.ShapeDtypeStruct((nseq * T, 1024), BF),
                   jax.ShapeDtypeStruct((nseq, 2, 8, 128, 128), F32)],
        scratch_shapes=[pltpu.VMEM((T, 128), F32), pltpu.VMEM((2, T, 128), F32), pltpu.VMEM((2, T, 128), F32),
                        pltpu.VMEM((2, 128, 128), F32)],
        compiler_params=_cp("arbitrary", "arbitrary"),
        name=f"ssd_scan_T{T}",
    )(dt_bias, a_log, dpar, xbc, xbc, xbc, z, z, h0)


def _rms_rows_kernel(x_ref, g_ref, o_ref):
    x = x_ref[...].astype(F32)
    o_ref[...] = (x * lax.rsqrt(jnp.mean(x * x, axis=-1, keepdims=True) + EPS) * g_ref[...]).astype(BF)


def _rms_rows(x, g):
    tm = 512
    n, w = x.shape
    return pl.pallas_call(
        _rms_rows_kernel,
        grid=(n // tm,),
        in_specs=[pl.BlockSpec((tm, w), lambda i: (i, 0)), pl.BlockSpec((1, w), lambda i: (0, 0))],
        out_specs=pl.BlockSpec((tm, w), lambda i: (i, 0)),
        out_shape=jax.ShapeDtypeStruct((n, w), BF),
        compiler_params=_cp("arbitrary"),
        name="ssd_out_norm",
    )(x, g.reshape(1, w))


def _rope_tables():
    t = np.arange(T_LAT)
    row = (t // GRID_W).astype(np.float32)
    col = (t % GRID_W).astype(np.float32)
    freqs = (ROPE_THETA ** (-np.arange(16, dtype=np.float32) / 16)).astype(np.float32)
    ang_r = row[:, None] * freqs[None, :]
    ang_c = col[:, None] * freqs[None, :]
    ang = np.concatenate([ang_r, ang_r, ang_c, ang_c], axis=-1).astype(np.float32)
    ang = np.concatenate([ang, ang], axis=-1)
    return jnp.cos(jnp.asarray(ang)), jnp.sin(jnp.asarray(ang))


def _rope(x, cos, sin):
    lane = lax.broadcasted_iota(jnp.int32, x.shape, 1)
    even_quarter = ((lane // 16) % 2) == 0
    partner = jnp.where(even_quarter, -pltpu.roll(x, 112, 1), pltpu.roll(x, 16, 1))
    return x * cos + partner * sin


def _mla_kv_kernel(src_ref, sm_ref, cos_ref, sin_ref, wkvb_ref, gkva_ref, gkn_ref, gkp_ref,
                   kh_ref, vh_ref, ckv_ref, *, norm_kv, rope):
    cc = src_ref[...].astype(F32)
    if norm_kv:
        cc = cc * lax.rsqrt(jnp.mean(cc * cc, axis=-1, keepdims=True) + EPS) * gkva_ref[...]
    ckv_ref[...] = cc
    kv = jnp.dot(cc.astype(BF), wkvb_ref[...], preferred_element_type=F32)
    sm = sm_ref[...].astype(F32)
    lane = lax.broadcasted_iota(jnp.int32, sm.shape, 1)
    pe = jnp.where(lane < 64, pltpu.roll(sm, 64, 1), 0.0)
    pe2 = jnp.sum(pe * pe, axis=-1, keepdims=True)
    for h in range(8):
        kn = kv[:, h * 256:h * 256 + 128]
        ri = lax.rsqrt((jnp.sum(kn * kn, axis=-1, keepdims=True) + pe2) * (1.0 / 192.0) + EPS)
        kh_ref[:, h * 256:h * 256 + 128] = (kn * ri * gkn_ref[...]).astype(BF)
        p = pe * ri * gkp_ref[...]
        if rope:
            p = _rope(p, cos_ref[...], sin_ref[...])
        kh_ref[:, h * 256 + 128:(h + 1) * 256] = p.astype(BF)
        vh_ref[:, h * 128:(h + 1) * 128] = kv[:, h * 256 + 128:(h + 1) * 256].astype(BF)


def _mla_kv(src, src_blk, sm, sm_blk, cos, sin, wkvb, gkva, gkn, gkp, *, rows, norm_kv, rope, name):
    tm = 256
    ntab = T_LAT // tm
    return pl.pallas_call(
        functools.partial(_mla_kv_kernel, norm_kv=norm_kv, rope=rope),
        grid=(rows // tm,),
        in_specs=[pl.BlockSpec((tm, 256), lambda i: (src_blk[0] + i, src_blk[1])),
                  pl.BlockSpec((tm, 128), lambda i: (sm_blk[0] + i, sm_blk[1])),
                  pl.BlockSpec((tm, 128), lambda i: (i % ntab, 0)),
                  pl.BlockSpec((tm, 128), lambda i: (i % ntab, 0)),
                  pl.BlockSpec((256, 2048), lambda i: (0, 0)),
                  pl.BlockSpec((1, 256), lambda i: (0, 0)),
                  pl.BlockSpec((1, 128), lambda i: (0, 0)),
                  pl.BlockSpec((1, 128), lambda i: (0, 0))],
        out_specs=[pl.BlockSpec((tm, 2048), lambda i: (i, 0)),
                   pl.BlockSpec((tm, 1024), lambda i: (i, 0)),
                   pl.BlockSpec((tm, 256), lambda i: (i, 0))],
        out_shape=[jax.ShapeDtypeStruct((rows, 2048), BF),
                   jax.ShapeDtypeStruct((rows, 1024), BF),
                   jax.ShapeDtypeStruct((rows, 256), F32)],
        compiler_params=_cp("arbitrary"),
        name=name,
    )(src, sm, cos, sin, wkvb, gkva, gkn, gkp)


def _mla_q_kernel(qa_ref, cos_ref, sin_ref, wqb_ref, gqa_ref, gq_ref, qh_ref, *, rope):
    qa = qa_ref[...].astype(F32)
    qa = qa * lax.rsqrt(jnp.mean(qa * qa, axis=-1, keepdims=True) + EPS) * gqa_ref[...]
    q = jnp.dot(qa.astype(BF), wqb_ref[...], preferred_element_type=F32)
    scale = 192.0 ** -0.5 * LOG2E
    for h in range(8):
        qn = q[:, h * 256:h * 256 + 128]
        qp = q[:, h * 256 + 128:(h + 1) * 256]
        ss = jnp.sum(qn * qn, axis=-1, keepdims=True) + jnp.sum(qp * qp, axis=-1, keepdims=True)
        ri = lax.rsqrt(ss * (1.0 / 192.0) + EPS)
        qh_ref[:, h * 256:h * 256 + 128] = (qn * ri * gq_ref[:, 0:128] * scale).astype(BF)
        p = qp * ri * gq_ref[:, 128:256]
        if rope:
            p = _rope(p, cos_ref[...], sin_ref[...])
        qh_ref[:, h * 256 + 128:(h + 1) * 256] = (p * scale).astype(BF)


def _mla_q(z, row_blk0, cos, sin, wqb, gqa, gq, *, rows, rope, name):
    tm = 256
    ntab = T_LAT // tm
    return pl.pallas_call(
        functools.partial(_mla_q_kernel, rope=rope),
        grid=(rows // tm,),
        in_specs=[pl.BlockSpec((tm, 768), lambda i: (row_blk0 + i, Z_QA // 768)),
                  pl.BlockSpec((tm, 128), lambda i: (i % ntab, 0)),
                  pl.BlockSpec((tm, 128), lambda i: (i % ntab, 0)),
                  pl.BlockSpec((768, 2048), lambda i: (0, 0)),
                  pl.BlockSpec((1, 768), lambda i: (0, 0)),
                  pl.BlockSpec((1, 256), lambda i: (0, 0))],
        out_specs=pl.BlockSpec((tm, 2048), lambda i: (i, 0)),
        out_shape=jax.ShapeDtypeStruct((rows, 2048), BF),
        compiler_params=_cp("arbitrary"),
        name=name,
    )(z, cos, sin, wqb, gqa, gq)


ATTN_ROWS = 256


def _mla_attn_kernel(q_ref, k_ref, v_ref, o_ref):
    for part in range(q_ref.shape[0] // ATTN_ROWS):
        rows = slice(part * ATTN_ROWS, (part + 1) * ATTN_ROWS)
        s = lax.dot_general(q_ref[rows, :], k_ref[...], NT, preferred_element_type=F32)
        p = jnp.exp2(s - jnp.max(s, axis=-1, keepdims=True))
        l = jnp.sum(p, axis=-1, keepdims=True)
        o = jnp.dot(p.astype(BF), v_ref[...], preferred_element_type=F32)
        o_ref[rows, :] = (o / l).astype(BF)


def _mla_attn(q, k, v, *, nb, tq_total, tk, name):
    tq = min(tq_total, 4 * ATTN_ROWS)
    nq = tq_total // tq
    return pl.pallas_call(
        _mla_attn_kernel,
        grid=(nb, 8, nq),
        in_specs=[pl.BlockSpec((tq, 256), lambda b, h, i: (b * nq + i, h)),
                  pl.BlockSpec((tk, 256), lambda b, h, i: (b, h)),
                  pl.BlockSpec((tk, 128), lambda b, h, i: (b, h))],
        out_specs=pl.BlockSpec((tq, 128), lambda b, h, i: (b * nq + i, h)),
        out_shape=jax.ShapeDtypeStruct((nb * tq_total, 1024), BF),
        compiler_params=_cp("arbitrary", "arbitrary", "arbitrary"),
        name=name,
    )(q, k, v)


def _group64_norm(x, g):
    r = lax.broadcasted_iota(jnp.int32, (128, 128), 0)
    c = lax.broadcasted_iota(jnp.int32, (128, 128), 1)
    gm = jnp.where((r // 64) == (c // 64), 1.0, 0.0).astype(BF)
    outs = []
    for j in range(8):
        xj = x[:, j * 128:(j + 1) * 128]
        sq = xj * xj
        hi = sq.astype(BF)
        lo = (sq - hi.astype(F32)).astype(BF)
        ms = (jnp.dot(hi, gm, preferred_element_type=F32) + jnp.dot(lo, gm, preferred_element_type=F32)) * (1.0 / 64.0)
        outs.append(xj * lax.rsqrt(ms + EPS) * g)
    return outs


def _diff_prep_kernel(q_ref, k_ref, cos_ref, sin_ref, gq_ref, gk_ref, qh_ref, kh_ref, kc_ref, *, rope):
    qs = _group64_norm(q_ref[...].astype(F32), gq_ref[...])
    ks = _group64_norm(k_ref[...].astype(F32), gk_ref[...])
    scale = 64.0 ** -0.5 * LOG2E
    for j in range(8):
        cols = slice(j * 128, (j + 1) * 128)
        qj, kj = qs[j], ks[j]
        kc_ref[:, cols] = kj
        if rope:
            qj = _rope(qj, cos_ref[...], sin_ref[...])
            kj = _rope(kj, cos_ref[...], sin_ref[...])
        qh_ref[:, cols] = (qj * scale).astype(BF)
        kh_ref[:, cols] = kj.astype(BF)


def _diff_prep(z, row_blk0, cos, sin, gq, gk, *, rows, rope, name):
    tm = 256
    ntab = T_LAT // tm
    return pl.pallas_call(
        functools.partial(_diff_prep_kernel, rope=rope),
        grid=(rows // tm,),
        in_specs=[pl.BlockSpec((tm, 1024), lambda i: (row_blk0 + i, Z_DQ // 1024)),
                  pl.BlockSpec((tm, 1024), lambda i: (row_blk0 + i, Z_DK // 1024)),
                  pl.BlockSpec((tm, 128), lambda i: (i % ntab, 0)),
                  pl.BlockSpec((tm, 128), lambda i: (i % ntab, 0)),
                  pl.BlockSpec((1, 128), lambda i: (0, 0)),
                  pl.BlockSpec((1, 128), lambda i: (0, 0))],
        out_specs=[pl.BlockSpec((tm, 1024), lambda i: (i, 0)),
                   pl.BlockSpec((tm, 1024), lambda i: (i, 0)),
                   pl.BlockSpec((tm, 1024), lambda i: (i, 0))],
        out_shape=[jax.ShapeDtypeStruct((rows, 1024), BF),
                   jax.ShapeDtypeStruct((rows, 1024), BF),
                   jax.ShapeDtypeStruct((rows, 1024), F32)],
        compiler_params=_cp("arbitrary"),
        name=name,
    )(z, z, cos, sin, gq, gk)


def _diff_attn_kernel(q_ref, k_ref, v_ref, lam_ref, g_ref, o_ref, *, lam_init):
    k = k_ref[...]
    v = v_ref[...]
    lam = lam_ref[...]
    lam_full = (jnp.exp(jnp.sum(lam[0:1, :] * lam[1:2, :], axis=-1, keepdims=True))
                - jnp.exp(jnp.sum(lam[2:3, :] * lam[3:4, :], axis=-1, keepdims=True)) + lam_init)
    for part in range(q_ref.shape[0] // ATTN_ROWS):
        rows = slice(part * ATTN_ROWS, (part + 1) * ATTN_ROWS)
        q = q_ref[rows, :]
        lane = lax.broadcasted_iota(jnp.int32, q.shape, 1)
        zero = jnp.zeros_like(q)
        outs = []
        for m in range(2):
            qm = jnp.where((lane < 64) if m == 0 else (lane >= 64), q, zero)
            s = lax.dot_general(qm, k, NT, preferred_element_type=F32)
            p = jnp.exp2(s - jnp.max(s, axis=-1, keepdims=True))
            l = jnp.sum(p, axis=-1, keepdims=True)
            outs.append(jnp.dot(p.astype(BF), v, preferred_element_type=F32) / l)
        od = outs[0] - lam_full * outs[1]
        y = od * lax.rsqrt(jnp.mean(od * od, axis=-1, keepdims=True) + EPS) * g_ref[...]
        o_ref[rows, :] = (y * (1.0 - lam_init)).astype(BF)


def _diff_attn(q, k, v, v_col0, lam, g, *, nb, tq_total, tk, lam_init, name):
    tq = min(tq_total, 4 * ATTN_ROWS)
    nq = tq_total // tq
    return pl.pallas_call(
        functools.partial(_diff_attn_kernel, lam_init=lam_init),
        grid=(nb, 8, nq),
        in_specs=[pl.BlockSpec((tq, 128), lambda b, h, i: (b * nq + i, h)),
                  pl.BlockSpec((tk, 128), lambda b, h, i: (b, h)),
                  pl.BlockSpec((tk, 128), lambda b, h, i: (b, v_col0 // 128 + h)),
                  pl.BlockSpec((4, 64), lambda b, h, i: (0, 0)),
                  pl.BlockSpec((1, 128), lambda b, h, i: (0, 0))],
        out_specs=pl.BlockSpec((tq, 128), lambda b, h, i: (b * nq + i, h)),
        out_shape=jax.ShapeDtypeStruct((nb * tq_total, 1024), BF),
        compiler_params=_cp("arbitrary", "arbitrary", "arbitrary"),
        name=name,
    )(q, k, v, lam, g)


PEER_TOPK = 16
_CAND_GROUPS = [(0, 16), (1, 8), (2, 5), (3, 4), (4, 3), (5, 2), (6, 2), (7, 2)]


def _top16_rows(s, payload=None):
    nrow = s.shape[0]
    row = lax.broadcasted_iota(jnp.int32, s.shape, 0)
    vals, idxs, pays = [], [], [[] for _ in (payload or [])]
    for _ in range(PEER_TOPK):
        m = jnp.max(s, axis=0, keepdims=True)
        i = jnp.min(jnp.where(s == m, row, nrow), axis=0, keepdims=True)
        hit = row == i
        vals.append(m)
        idxs.append(i)
        for a, pay in enumerate(payload or []):
            pays[a].append(jnp.max(jnp.where(hit, pay, -1), axis=0, keepdims=True))
        s = jnp.where(hit, -jnp.inf, s)
    return (jnp.concatenate(vals, axis=0), jnp.concatenate(idxs, axis=0),
            [jnp.concatenate(p, axis=0) for p in pays])


def _peer_route_kernel(q_ref, keys_ref, w_ref, a_s, b_s, g_s, *, tn):
    sub = lax.broadcasted_iota(jnp.int32, (8, tn), 0)
    for h in range(8):
        tops = []
        for p in range(2):
            qhp = q_ref[:, (h * 2 + p) * 128:(h * 2 + p + 1) * 128]
            st = lax.dot_general(keys_ref[h, p], qhp, NT, preferred_element_type=F32)
            v, i, _ = _top16_rows(st)
            tops.append((v, i))
        (s1, i1), (s2, i2) = tops
        cs, ca, cb = [], [], []
        for r, nvalid in _CAND_GROUPS:
            width = 16 if r == 0 else 8
            blk = s1[r:r + 1, :] + s2[0:width, :]
            if nvalid < width:
                blk = jnp.where(sub < nvalid, blk, -jnp.inf)
            cs.append(blk)
            ca.append(jnp.broadcast_to(i1[r:r + 1, :], (width, tn)))
            cb.append(i2[0:width, :])
        cs.append(s1[8:16, :] + s2[0:1, :])
        ca.append(i1[8:16, :])
        cb.append(jnp.broadcast_to(i2[0:1, :], (8, tn)))
        cand_s = jnp.concatenate(cs, axis=0)
        cand_a = jnp.concatenate(ca, axis=0)
        cand_b = jnp.concatenate(cb, axis=0)
        top_s, _, (sel_a, sel_b) = _top16_rows(cand_s, [cand_a, cand_b])
        e = jnp.exp(top_s - top_s[0:1, :])
        g = e / jnp.sum(e, axis=0, keepdims=True)
        a_s[:, h * 16:(h + 1) * 16] = sel_a.astype(F32).T
        b_s[:, h * 16:(h + 1) * 16] = sel_b.astype(F32).T
        g_s[:, h * 16:(h + 1) * 16] = g.T

    ids = lax.broadcasted_iota(jnp.int32, (128, 128), 0).astype(F32)

    def build(t, carry):
        n0 = pl.multiple_of(t * 16, 16)
        a_rows = a_s[pl.ds(n0, 16), :]
        b_rows = b_s[pl.ds(n0, 16), :]
        g_rows = g_s[pl.ds(n0, 16), :]
        ws = []
        for r in range(16):
            pt = jnp.where(ids == a_rows[r:r + 1, :], g_rows[r:r + 1, :], 0.0).astype(BF)
            qt = jnp.where(ids == b_rows[r:r + 1, :], 1.0, 0.0).astype(BF)
            ws.append(lax.dot_general(pt, qt, NT, preferred_element_type=F32))
        w_ref[:, pl.ds(n0, 16), :] = jnp.swapaxes(jnp.stack(ws, axis=0), 0, 1).astype(BF)
        return carry

    lax.fori_loop(0, tn // 16, build, 0)


def _peer_route(q, keys, layer):
    tn = 128
    n = q.shape[0]
    return pl.pallas_call(
        functools.partial(_peer_route_kernel, tn=tn),
        grid=(n // tn,),
        in_specs=[pl.BlockSpec((tn, 2048), lambda i: (i, 0)),
                  pl.BlockSpec((None, 8, 2, 128, 128), lambda i: (layer, 0, 0, 0, 0))],
        out_specs=pl.BlockSpec((128, tn, 128), lambda i: (0, i, 0)),
        out_shape=jax.ShapeDtypeStruct((128, n, 128), BF),
        scratch_shapes=[pltpu.VMEM((tn, 128), F32), pltpu.VMEM((tn, 128), F32), pltpu.VMEM((tn, 128), F32)],
        compiler_params=_cp("arbitrary"),
        name="peer_route",
    )(q, keys)


def _gelu_tanh(x):
    return 0.5 * x * (1.0 + jnp.tanh(0.7978845608028654 * (x + 0.044715 * x * x * x)))


def _peer_dense_kernel(x_ref, u_ref, v_ref, w_ref, h_ref, mod_ref, o_ref):
    @pl.when(pl.program_id(1) == 0)
    def _():
        o_ref[...] = jnp.zeros_like(o_ref)

    hid = lax.dot_general(x_ref[...], u_ref[...], NT, preferred_element_type=F32)
    w = jnp.concatenate([w_ref[a] for a in range(w_ref.shape[0])], axis=1)
    act = (_gelu_tanh(hid) * w.astype(F32)).astype(BF)
    o_ref[...] += jnp.dot(act, v_ref[...], preferred_element_type=F32)

    @pl.when(pl.program_id(1) == pl.num_programs(1) - 1)
    def _():
        o_ref[...] = h_ref[...] + mod_ref[0][5:6, :] * o_ref[...]


def _peer_dense(x, u, v, w, layer, h, mod3):
    tn, te = 512, 512
    n = x.shape[0]
    ne = u.shape[1]
    return pl.pallas_call(
        _peer_dense_kernel,
        grid=(n // tn, ne // te),
        in_specs=[pl.BlockSpec((tn, D), lambda i, e: (i, 0), pipeline_mode=pl.Buffered(1)),
                  pl.BlockSpec((None, te, D), lambda i, e: (layer, e, 0)),
                  pl.BlockSpec((None, te, D), lambda i, e: (layer, e, 0)),
                  pl.BlockSpec((te // 128, tn, 128), lambda i, e: (e, i, 0)),
                  pl.BlockSpec((tn, D), lambda i, e: (i, 0), pipeline_mode=pl.Buffered(1)),
                  pl.BlockSpec((1, 6, D), lambda i, e: (i * tn // MOD_ROWS, 0, 0))],
        out_specs=pl.BlockSpec((tn, D), lambda i, e: (i, 0), pipeline_mode=pl.Buffered(1)),
        out_shape=jax.ShapeDtypeStruct((n, D), F32),
        compiler_params=_cp("arbitrary", "arbitrary"),
        name="peer_dense",
    )(x, u, v, w, h, mod3)


def _permute_w_in(w):
    o = dict(a_q=0, a_k=512, a_v=1024, a_g=2048, a_gk=3072, b_z=3104, b_xbc=4128, b_dt=5664,
             c_qa=5680, c_kva=6448, c_kpe=6704, d_q=6768, d_k=7792, d_v=8816)

    def seg(name, width):
        return w[..., o[name]:o[name] + width]

    zeros = lambda n: jnp.zeros(w.shape[:-1] + (n,), w.dtype)
    parts = [seg('a_v', 1024), seg('a_g', 1024), seg('b_z', 1024), seg('d_q', 1024), seg('d_k', 1024),
             seg('d_v', 1024), seg('b_xbc', 1536), seg('a_q', 512), seg('a_k', 512), seg('c_kva', 256),
             seg('a_gk', 32), seg('b_dt', 16), zeros(16), seg('c_kpe', 64), zeros(128), seg('c_qa', 768)]
    return jnp.concatenate(parts, axis=-1).astype(BF)


def _permute_w_qb(w):
    w3 = w.reshape(w.shape[0], 8, 192)
    return jnp.concatenate([w3, jnp.zeros((w.shape[0], 8, 64), w.dtype)], axis=-1).reshape(w.shape[0], 2048).astype(BF)


def kernel(x_prompt, x_sample, cache_mla_ckv, cache_mla_kpe, cache_diff_k, cache_diff_v, state_gla, state_ssd, c, c_ctx, norm1_g, norm2_g, w_ada, b_ada, w_in, w_out, gla_w_gk, gla_b_gk, gla_norm_g, ssd_conv_w, ssd_conv_b, ssd_a_log, ssd_dt_bias, ssd_d, ssd_norm_g, mla_qa_norm_g, mla_w_qb, mla_kva_norm_g, mla_w_kvb, mla_q_norm_g, mla_k_norm_g, diff_q_norm_g, diff_k_norm_g, diff_lambda, diff_subln_g, peer_w_q, peer_sub_keys, peer_u, peer_v):
    depth = w_in.shape[0]
    w_in_b = _permute_w_in(w_in)
    w_out_b = w_out.astype(BF)
    peer_wq_b = peer_w_q.astype(BF)
    peer_keys_b = peer_sub_keys.astype(BF)
    peer_u_b = peer_u.astype(BF)
    peer_v_b = peer_v.astype(BF)
    x = jnp.concatenate([x_prompt.reshape(N_CTX, D), x_sample.reshape(N_LAT, D)], axis=0)
    cv8 = jnp.concatenate([c_ctx[None, :], c, jnp.zeros((5, D), F32)], axis=0)
    cos_t, sin_t = _rope_tables()
    nct = N_CTX // T_CTX
    outs = {k: [] for k in ('ckv', 'kpe', 'dk', 'dv', 'gla', 'ssd')}

    for l in range(depth):
        mod3 = _modulation(cv8, w_ada, b_ada, l).reshape(8, 6, D)[:3]

        (z,) = _norm_mm(x, norm1_g[l], mod3, w_in_b, l, shift_row=0, tn=768, emit_xn=False, name="norm_in_proj")

        wgk = gla_w_gk[l]
        wgk_pad = jnp.zeros((2, 128, 512), F32)
        wgk_pad = wgk_pad.at[0, SM_GK:SM_GK + 16].set(wgk[0]).at[1, SM_GK + 16:SM_GK + 32].set(wgk[1]).astype(BF)
        bgk = gla_b_gk[l].reshape(2, 1, 512)
        ng = gla_norm_g[l].reshape(1, 128)
        a_ctx, gla_fin = _gla(z, wgk_pad, bgk, ng, jnp.zeros((B_CTX, 2, 8, 64, 128), F32),
                              nseq=B_CTX, T=T_CTX, row0=0)
        a_lat, _ = _gla(z, wgk_pad, bgk, ng, state_gla[:, l], nseq=B_LAT, T=T_LAT, row0=N_CTX // T_LAT)
        out_a = jnp.concatenate([a_ctx, a_lat], axis=0)

        conv_b = ssd_conv_b[l].reshape(1, 1536)
        xbc_ctx = _conv(z, ssd_conv_w[l], conv_b, nseq=B_CTX, T=T_CTX, row0=0)
        xbc_lat = _conv(z, ssd_conv_w[l], conv_b, nseq=B_LAT, T=T_LAT, row0=N_CTX // T_LAT)
        dpar = ssd_d[l].reshape(1, 8)
        y_ctx, ssd_fin = _ssd(xbc_ctx, z, ssd_dt_bias[l], ssd_a_log[l], dpar,
                              jnp.zeros((B_CTX, 2, 8, 128, 128), F32), nseq=B_CTX, T=T_CTX, row0=0)
        y_lat, _ = _ssd(xbc_lat, z, ssd_dt_bias[l], ssd_a_log[l], dpar, state_ssd[:, l],
                        nseq=B_LAT, T=T_LAT, row0=N_CTX // T_LAT)
        out_b = _rms_rows(jnp.concatenate([y_ctx, y_lat], axis=0), ssd_norm_g[l])

        wkvb = mla_w_kvb[l].astype(BF)
        wqb = _permute_w_qb(mla_w_qb[l])
        gkva = mla_kva_norm_g[l].reshape(1, 256)
        gk = mla_k_norm_g[l]
        gkn = gk[:128].reshape(1, 128)
        gkp = jnp.concatenate([gk[128:], jnp.zeros((64,), F32)]).reshape(1, 128)
        gq = jnp.concatenate([mla_q_norm_g[l], jnp.zeros((64,), F32)]).reshape(1, 256)
        gqa = mla_qa_norm_g[l].reshape(1, 768)
        kh_ctx, vh_ctx, ckv_ctx = _mla_kv(z, (0, Z_KVA // 256), z, (0, Z_SM // 128), cos_t, sin_t, wkvb, gkva, gkn, gkp,
                                          rows=N_CTX, norm_kv=True, rope=False, name="mla_kv_ctx")
        kh_lat, vh_lat, _ = _mla_kv(z, (N_CTX // 256, Z_KVA // 256), z, (N_CTX // 256, Z_SM // 128), cos_t, sin_t,
                                    wkvb, gkva, gkn, gkp, rows=N_LAT, norm_kv=True, rope=True, name="mla_kv_lat")
        cache_sm = jnp.concatenate([jnp.zeros((B_LAT * PAST, 64), F32),
                                    cache_mla_kpe[:, l].reshape(B_LAT * PAST, 64)], axis=1)
        kh_past, vh_past, _ = _mla_kv(cache_mla_ckv[:, l].reshape(B_LAT * PAST, 256), (0, 0), cache_sm, (0, 0),
                                      cos_t, sin_t, wkvb, gkva, gkn, gkp,
                                      rows=B_LAT * PAST, norm_kv=False, rope=False, name="mla_kv_past")
        qh_ctx = _mla_q(z, 0, cos_t, sin_t, wqb, gqa, gq, rows=N_CTX, rope=False, name="mla_q_ctx")
        qh_lat = _mla_q(z, N_CTX // 256, cos_t, sin_t, wqb, gqa, gq, rows=N_LAT, rope=True, name="mla_q_lat")
        tk_lat = PAST + T_LAT
        k_lat = jnp.concatenate([kh_past.reshape(B_LAT, PAST, 2048), kh_lat.reshape(B_LAT, T_LAT, 2048)],
                                axis=1).reshape(B_LAT * tk_lat, 2048)
        v_lat = jnp.concatenate([vh_past.reshape(B_LAT, PAST, 1024), vh_lat.reshape(B_LAT, T_LAT, 1024)],
                                axis=1).reshape(B_LAT * tk_lat, 1024)
        c_ctx_o = _mla_attn(qh_ctx, kh_ctx, vh_ctx, nb=B_CTX, tq_total=T_CTX, tk=T_CTX, name="mla_attn_ctx")
        c_lat_o = _mla_attn(qh_lat, k_lat, v_lat, nb=B_LAT, tq_total=T_LAT, tk=tk_lat, name="mla_attn_lat")
        out_c = jnp.concatenate([c_ctx_o, c_lat_o], axis=0)

        gdq = jnp.tile(diff_q_norm_g[l], 2).reshape(1, 128)
        gdk = jnp.tile(diff_k_norm_g[l], 2).reshape(1, 128)
        dq_ctx, dk_ctx, dkc_ctx = _diff_prep(z, 0, cos_t, sin_t, gdq, gdk, rows=N_CTX, rope=False, name="diff_prep_ctx")
        dq_lat, dk_lat, _ = _diff_prep(z, N_CTX // 256, cos_t, sin_t, gdq, gdk, rows=N_LAT, rope=True,
                                       name="diff_prep_lat")
        dk_full = jnp.concatenate([cache_diff_k[:, l].reshape(B_LAT, PAST, 1024).astype(BF),
                                   dk_lat.reshape(B_LAT, T_LAT, 1024)], axis=1).reshape(B_LAT * tk_lat, 1024)
        dv_full = jnp.concatenate([cache_diff_v[:, l].reshape(B_LAT, PAST, 1024).astype(BF),
                                   z[N_CTX:, Z_DV:Z_DV + 1024].reshape(B_LAT, T_LAT, 1024)],
                                  axis=1).reshape(B_LAT * tk_lat, 1024)
        lam_init = 0.8 - 0.6 * math.exp(-0.3 * l)
        gsub = diff_subln_g[l].reshape(1, 128)
        d_ctx_o = _diff_attn(dq_ctx, dk_ctx, z, Z_DV, diff_lambda[l], gsub, nb=B_CTX, tq_total=T_CTX, tk=T_CTX,
                             lam_init=lam_init, name="diff_attn_ctx")
        d_lat_o = _diff_attn(dq_lat, dk_full, dv_full, 0, diff_lambda[l], gsub, nb=B_LAT, tq_total=T_LAT, tk=tk_lat,
                             lam_init=lam_init, name="diff_attn_lat")
        out_d = jnp.concatenate([d_ctx_o, d_lat_o], axis=0)

        hmid = _out_proj((out_a, out_b, out_c, out_d), w_out_b, l, x, mod3)

        pq, u2 = _norm_mm(hmid, norm2_g[l], mod3, peer_wq_b, l, shift_row=3, tn=512, emit_xn=True,
                          name="norm_peer_query")
        wmap = _peer_route(pq, peer_keys_b, l)
        x = _peer_dense(u2, peer_u_b, peer_v_b, wmap, l, hmid, mod3)

        outs['ckv'].append(ckv_ctx.reshape(B_CTX, T_CTX, 256))
        outs['kpe'].append(z[:N_CTX, Z_SM + SM_KPE:Z_SM + 128].astype(F32).reshape(B_CTX, T_CTX, 64))
        outs['dk'].append(dkc_ctx.reshape(B_CTX, T_CTX, 8, 2, 64))
        outs['dv'].append(z[:N_CTX, Z_DV:Z_DV + 1024].astype(F32).reshape(B_CTX, T_CTX, 8, 128))
        outs['gla'].append(gla_fin)
        outs['ssd'].append(ssd_fin)

    y_p = x[:N_CTX].reshape(B_CTX, T_CTX, D)
    y_s = x[N_CTX:].reshape(B_LAT, T_LAT, D)
    return (y_p, y_s, jnp.stack(outs['ckv'], axis=1), jnp.stack(outs['kpe'], axis=1),
            jnp.stack(outs['dk'], axis=1), jnp.stack(outs['dv'], axis=1),
            jnp.stack(outs['gla'], axis=1), jnp.stack(outs['ssd'], axis=1))
```

```python
import functools
import math

import jax
import jax.numpy as jnp
import numpy as np
from jax import lax
from jax.experimental import pallas as pl
from jax.experimental.pallas import tpu as pltpu

F32 = jnp.float32
BF = jnp.bfloat16
LOG2E = 1.4426950408889634

D = 4096
EPS = 1e-6
N_CTX = 4096
N_LAT = 8192
N_TOK = N_CTX + N_LAT
MOD_ROWS = 4096
T_CTX, B_CTX = 256, 16
T_LAT, B_LAT = 4096, 2
PAST = 512
GRID_W = 64
ROPE_THETA = 10000.0

V7X_VMEM_LIMIT = 56 * 1024 * 1024

Z_AV, Z_AG, Z_BZ, Z_DQ, Z_DK, Z_DV = 0, 1024, 2048, 3072, 4096, 5120
Z_XBC, Z_AQ, Z_AK, Z_KVA, Z_SM, Z_QA = 6144, 7680, 8192, 8704, 8960, 9216
Z_W = 9984
SM_GK, SM_DT, SM_KPE = 0, 32, 64

NT = (((1,), (1,)), ((), ()))
TN = (((0,), (0,)), ((), ()))


def _cp(*sem):
    return pltpu.CompilerParams(dimension_semantics=sem, vmem_limit_bytes=V7X_VMEM_LIMIT)


def _sigmoid(x):
    return 1.0 / (1.0 + jnp.exp(-x))


def _silu(x):
    return x * _sigmoid(x)


def _softplus(x):
    return jnp.maximum(x, 0.0) + jnp.log1p(jnp.exp(-jnp.abs(x)))


def _log_sigmoid(x):
    return jnp.minimum(x, 0.0) - jnp.log1p(jnp.exp(-jnp.abs(x)))


def _block_cumsum(x, blk, reverse):
    n = x.shape[0]
    row = lax.broadcasted_iota(jnp.int32, x.shape, 0) % blk
    step = 1
    while step < blk:
        if reverse:
            x = x + jnp.where(row < blk - step, pltpu.roll(x, n - step, 0), 0.0)
        else:
            x = x + jnp.where(row >= step, pltpu.roll(x, step, 0), 0.0)
        step *= 2
    return x


def _mod_kernel(c_ref, w_ref, b_ref, o_ref):
    a = _silu(c_ref[...])
    o_ref[...] = jnp.dot(a.astype(BF), w_ref[...].astype(BF), preferred_element_type=F32) + b_ref[...]


def _modulation(cv8, w_ada, b_ada, layer):
    tn = 512
    return pl.pallas_call(
        _mod_kernel,
        grid=(6 * D // tn,),
        in_specs=[pl.BlockSpec((8, D), lambda j: (0, 0)),
                  pl.BlockSpec((None, D, tn), lambda j: (layer, 0, j)),
                  pl.BlockSpec((None, 1, tn), lambda j: (layer, 0, j))],
        out_specs=pl.BlockSpec((8, tn), lambda j: (0, j)),
        out_shape=jax.ShapeDtypeStruct((8, 6 * D), F32),
        compiler_params=_cp("arbitrary"),
        name="modulation",
    )(cv8, w_ada, b_ada.reshape(b_ada.shape[0], 1, 6 * D))


def _norm_mm_kernel(x_ref, g_ref, mod_ref, w_ref, o_ref, *rest, shift_row, emit_xn):
    if emit_xn:
        xn_out_ref, xn = rest
    else:
        (xn,) = rest

    @pl.when(pl.program_id(1) == 0)
    def _():
        m = mod_ref[0]
        gain = g_ref[...] * (1.0 + m[shift_row + 1:shift_row + 2, :])
        shift = m[shift_row:shift_row + 1, :]

        def chunk(r, carry):
            rows = pl.ds(pl.multiple_of(r * 64, 64), 64)
            x = x_ref[rows, :]
            u = (x * lax.rsqrt(jnp.mean(x * x, axis=-1, keepdims=True) + EPS) * gain + shift).astype(BF)
            xn[rows, :] = u
            if emit_xn:
                xn_out_ref[rows, :] = u
            return carry

        lax.fori_loop(0, x_ref.shape[0] // 64, chunk, 0)

    o_ref[...] = jnp.dot(xn[...], w_ref[...], preferred_element_type=F32).astype(o_ref.dtype)


def _norm_mm(x, g, mod3, w, layer, *, shift_row, tn, emit_xn, name):
    tm = 512
    n, k = x.shape
    nout = w.shape[2]
    out_shape = [jax.ShapeDtypeStruct((n, nout), BF)]
    out_specs = [pl.BlockSpec((tm, tn), lambda i, j: (i, j))]
    if emit_xn:
        out_shape.append(jax.ShapeDtypeStruct((n, k), BF))
        out_specs.append(pl.BlockSpec((tm, k), lambda i, j: (i, 0)))
    res = pl.pallas_call(
        functools.partial(_norm_mm_kernel, shift_row=shift_row, emit_xn=emit_xn),
        grid=(n // tm, nout // tn),
        in_specs=[pl.BlockSpec((tm, k), lambda i, j: (i, 0)),
                  pl.BlockSpec((1, k), lambda i, j: (0, 0)),
                  pl.BlockSpec((1, 6, k), lambda i, j: (i * tm // MOD_ROWS, 0, 0)),
                  pl.BlockSpec((None, k, tn), lambda i, j: (layer, 0, j))],
        out_specs=out_specs,
        out_shape=out_shape,
        scratch_shapes=[pltpu.VMEM((tm, k), BF)],
        compiler_params=_cp("arbitrary", "arbitrary"),
        name=name,
    )(x, g.reshape(1, k), mod3, w)
    return res


def _out_proj_kernel(a_ref, b_ref, c_ref, d_ref, w_ref, x_ref, mod_ref, o_ref):
    acc = jnp.dot(a_ref[...], w_ref[0:1024, :], preferred_element_type=F32)
    acc += jnp.dot(b_ref[...], w_ref[1024:2048, :], preferred_element_type=F32)
    acc += jnp.dot(c_ref[...], w_ref[2048:3072, :], preferred_element_type=F32)
    acc += jnp.dot(d_ref[...], w_ref[3072:4096, :], preferred_element_type=F32)
    o_ref[...] = x_ref[...] + mod_ref[0][2:3, :] * acc


def _out_proj(mix, w, layer, x, mod3):
    tm, tn = 512, 1024
    n = x.shape[0]
    mspec = pl.BlockSpec((tm, 1024), lambda i, j: (i, 0))
    return pl.pallas_call(
        _out_proj_kernel,
        grid=(n // tm, D // tn),
        in_specs=[mspec, mspec, mspec, mspec,
                  pl.BlockSpec((None, D, tn), lambda i, j: (layer, 0, j)),
                  pl.BlockSpec((tm, tn), lambda i, j: (i, j)),
                  pl.BlockSpec((1, 6, tn), lambda i, j: (i * tm // MOD_ROWS, 0, j))],
        out_specs=pl.BlockSpec((tm, tn), lambda i, j: (i, j)),
        out_shape=jax.ShapeDtypeStruct((n, D), F32),
        compiler_params=_cp("arbitrary", "arbitrary"),
        name="out_proj",
    )(*mix, w, x, mod3)


GLA_BLK = 16


def _gla_kernel(q_ref, k_ref, v_ref, g_ref, sm_ref, wgk_ref, bgk_ref, ng_ref, s0_ref,
                o_ref, sf_ref, oacc, cumf, cumb, st, *, T):
    nb = T // GLA_BLK

    def gate_body(r, carry):
        rows = pl.ds(pl.multiple_of(r * 256, 256), 256)
        smb = sm_ref[rows, :]
        gf = jnp.dot(smb, wgk_ref[0], preferred_element_type=F32) + bgk_ref[0]
        gb = jnp.dot(smb, wgk_ref[1], preferred_element_type=F32) + bgk_ref[1]
        laf = _log_sigmoid(gf) * (1.0 / 16.0)
        lab = _log_sigmoid(gb) * (1.0 / 16.0)
        cumf[rows, :] = _block_cumsum(laf, GLA_BLK, reverse=False)
        cumb[rows, :] = _block_cumsum(lab, GLA_BLK, reverse=True)
        oacc[rows, :] = jnp.zeros((256, 256), F32)
        return carry

    lax.fori_loop(0, T // 256, gate_body, 0)

    for d in range(2):
        for h in range(2):
            st[d, h] = s0_ref[0, d, h].T

    scale = 64.0 ** -0.5
    rowi = lax.broadcasted_iota(jnp.int32, (GLA_BLK, 128), 0)
    sel_r = lax.broadcasted_iota(jnp.int32, (128, 256), 0)
    sel_c = lax.broadcasted_iota(jnp.int32, (128, 256), 1)
    head_sel = jnp.where((sel_r // 64) == (sel_c // 128), 1.0, 0.0).astype(BF)

    def body(i, carry):
        for d in range(2):
            blk = i if d == 0 else nb - 1 - i
            rows = pl.ds(pl.multiple_of(blk * GLA_BLK, GLA_BLK), GLA_BLK)
            qb = q_ref[rows, :].astype(F32) * scale
            kb = k_ref[rows, :].astype(F32)
            vb = v_ref[rows, :]
            vf = vb.astype(F32)
            cb = cumf[rows, :] if d == 0 else cumb[rows, :]
            edge = cb[GLA_BLK - 1:GLA_BLK, :] if d == 0 else cb[0:1, :]
            qh = qb * jnp.exp(cb)
            kt = kb * jnp.exp(edge - cb)
            dec = jnp.exp(edge)
            ws = []
            for s in range(GLA_BLK):
                msk = (rowi >= s) if d == 0 else (rowi <= s)
                w = qb * kb[s:s + 1, :] * jnp.exp(jnp.minimum(cb - cb[s:s + 1, :], 0.0))
                ws.append(jnp.where(msk, w, 0.0))
            wst = jnp.concatenate(ws, axis=0).astype(BF)
            att = jnp.dot(wst, head_sel, preferred_element_type=F32)
            od = jnp.zeros((GLA_BLK, 256), F32)
            for s in range(GLA_BLK):
                od = od + att[s * GLA_BLK:(s + 1) * GLA_BLK, :] * vf[s:s + 1, :]
            od0 = od[:, 0:128]
            od1 = od[:, 128:256]
            for h in range(2):
                hs = slice(h * 64, h * 64 + 64)
                s_t = st[d, h]
                o_h = lax.dot_general(qh[:, hs].astype(BF), s_t.astype(BF), NT, preferred_element_type=F32)
                o_h = o_h + (od0 if h == 0 else od1)
                upd = lax.dot_general(vb[:, h * 128:(h + 1) * 128], kt[:, hs].astype(BF), TN,
                                      preferred_element_type=F32)
                st[d, h] = s_t * dec[:, hs] + upd
                oacc[rows, h * 128:(h + 1) * 128] += o_h
        return carry

    lax.fori_loop(0, nb, body, 0, unroll=2)

    for d in range(2):
        for h in range(2):
            sf_ref[0, d, h] = st[d, h].T

    def epi(r, carry):
        rows = pl.ds(pl.multiple_of(r * 256, 256), 256)
        for h in range(2):
            cols = slice(h * 128, (h + 1) * 128)
            o = oacc[rows, cols]
            y = o * lax.rsqrt(jnp.mean(o * o, axis=-1, keepdims=True) + EPS) * ng_ref[...]
            o_ref[rows, cols] = (y * _silu(g_ref[rows, cols].astype(F32))).astype(BF)
        return carry

    lax.fori_loop(0, T // 256, epi, 0)


def _gla(z, wgk_pad, bgk, ng, s0, *, nseq, T, row0):
    return pl.pallas_call(
        functools.partial(_gla_kernel, T=T),
        grid=(nseq, 4),
        in_specs=[pl.BlockSpec((T, 128), lambda s, p: (row0 + s, Z_AQ // 128 + p)),
                  pl.BlockSpec((T, 128), lambda s, p: (row0 + s, Z_AK // 128 + p)),
                  pl.BlockSpec((T, 256), lambda s, p: (row0 + s, Z_AV // 256 + p)),
                  pl.BlockSpec((T, 256), lambda s, p: (row0 + s, Z_AG // 256 + p)),
                  pl.BlockSpec((T, 128), lambda s, p: (row0 + s, Z_SM // 128)),
                  pl.BlockSpec((2, 128, 128), lambda s, p: (0, 0, p)),
                  pl.BlockSpec((2, 1, 128), lambda s, p: (0, 0, p)),
                  pl.BlockSpec((1, 128), lambda s, p: (0, 0)),
                  pl.BlockSpec((1, 2, 2, 64, 128), lambda s, p: (s, 0, p, 0, 0))],
        out_specs=[pl.BlockSpec((T, 256), lambda s, p: (s, p)),
                   pl.BlockSpec((1, 2, 2, 64, 128), lambda s, p: (s, 0, p, 0, 0))],
        out_shape=[jax.ShapeDtypeStruct((nseq * T, 1024), BF),
                   jax.ShapeDtypeStruct((nseq, 2, 8, 64, 128), F32)],
        scratch_shapes=[pltpu.VMEM((T, 256), F32), pltpu.VMEM((T, 128), F32), pltpu.VMEM((T, 128), F32),
                        pltpu.VMEM((2, 2, 128, 64), F32)],
        compiler_params=_cp("arbitrary", "arbitrary"),
        name=f"gla_T{T}",
    )(z, z, z, z, z, wgk_pad, bgk, ng, s0)


def _conv_kernel(x_ref, w_ref, b_ref, o_ref, xp, *, T):
    xp[0:8, :] = jnp.zeros((8, 256), F32)
    xp[8 + T:16 + T, :] = jnp.zeros((8, 256), F32)
    xp[8:8 + T, :] = x_ref[...].astype(F32)
    for r in range(T // 256):
        acc = b_ref[...] + w_ref[0:1, :] * xp[6 + r * 256:6 + (r + 1) * 256, :]
        for kk in range(1, 5):
            acc = acc + w_ref[kk:kk + 1, :] * xp[6 + kk + r * 256:6 + kk + (r + 1) * 256, :]
        o_ref[r * 256:(r + 1) * 256, :] = _silu(acc).astype(BF)


def _conv(z, w, b, *, nseq, T, row0):
    return pl.pallas_call(
        functools.partial(_conv_kernel, T=T),
        grid=(nseq, 6),
        in_specs=[pl.BlockSpec((T, 256), lambda s, j: (row0 + s, Z_XBC // 256 + j)),
                  pl.BlockSpec((5, 256), lambda s, j: (0, j)),
                  pl.BlockSpec((1, 256), lambda s, j: (0, j))],
        out_specs=pl.BlockSpec((T, 256), lambda s, j: (s, j)),
        out_shape=jax.ShapeDtypeStruct((nseq * T, 1536), BF),
        scratch_shapes=[pltpu.VMEM((T + 16, 256), F32)],
        compiler_params=_cp("arbitrary", "arbitrary"),
        name=f"ssd_conv_T{T}",
    )(z, w, b)


SSD_CHUNK = 128


def _ssd_kernel(bias_ref, alog_ref, dpar_ref, x_ref, b_ref, c_ref, z_ref, sm_ref, h0_ref,
                y_ref, hf_ref, yacc, dts, cums, hst, *, T):
    hp = pl.program_id(1)
    nc = T // SSD_CHUNK
    r = lax.broadcasted_iota(jnp.int32, (128, 128), 0)
    c = lax.broadcasted_iota(jnp.int32, (128, 128), 1)
    for hh in range(2):
        hst[0, hh] = h0_ref[0, 0, hh]
        hst[1, hh] = h0_ref[0, 1, hh]

    def prep(ci, carry):
        rows = pl.ds(pl.multiple_of(ci * SSD_CHUNK, SSD_CHUNK), SSD_CHUNK)
        yacc[rows, :] = jnp.zeros((SSD_CHUNK, 256), F32)
        for hh in range(2):
            h = hp * 2 + hh
            for d in range(2):
                sel = jnp.where(r == SM_DT + d * 8 + h, 1.0, 0.0).astype(BF)
                raw = jnp.dot(sm_ref[rows, :], sel, preferred_element_type=F32)
                dtb = _softplus(raw + bias_ref[d, h])
                a_neg = -jnp.exp(jnp.full((1, 128), alog_ref[d, h], F32))
                dts[d, hh, rows, :] = dtb
                cums[d, hh, rows, :] = _block_cumsum(dtb * a_neg, SSD_CHUNK, reverse=(d == 1))
        return carry

    lax.fori_loop(0, nc, prep, 0, unroll=2)

    def body(ci, carry):
        for d in range(2):
            blk = ci if d == 0 else nc - 1 - ci
            rows = pl.ds(pl.multiple_of(blk * SSD_CHUNK, SSD_CHUNK), SSD_CHUNK)
            cm = c_ref[rows, :]
            bm = b_ref[rows, :]
            cbt = lax.dot_general(cm, bm, NT, preferred_element_type=F32)
            msk = (r >= c) if d == 0 else (r <= c)
            for hh in range(2):
                cols = slice(hh * 128, (hh + 1) * 128)
                dtb = dts[d, hh, rows, :]
                cb = cums[d, hh, rows, :]
                c_t = cb.T
                dt_t = dtb.T
                seg = jnp.exp(jnp.where(msk, cb - c_t, -jnp.inf))
                xb = x_ref[rows, cols]
                scores = cbt * seg * dt_t
                hs = hst[d, hh]
                y = jnp.dot(scores.astype(BF), xb, preferred_element_type=F32)
                y = y + lax.dot_general(cm, hs.astype(BF), NT, preferred_element_type=F32) * jnp.exp(cb)
                edge = cb[SSD_CHUNK - 1:SSD_CHUNK, :] if d == 0 else cb[0:1, :]
                wgt = jnp.exp(edge - cb) * dtb
                bw = (bm.astype(F32) * wgt).astype(BF)
                hst[d, hh] = jnp.exp(edge) * hs + lax.dot_general(xb, bw, TN, preferred_element_type=F32)
                yacc[rows, cols] += y
        return carry

    lax.fori_loop(0, nc, body, 0, unroll=2)
    for hh in range(2):
        hf_ref[0, 0, hh] = hst[0, hh]
        hf_ref[0, 1, hh] = hst[1, hh]

    def epi(ri, carry):
        rows = pl.ds(pl.multiple_of(ri * 256, 256), 256)
        for hh in range(2):
            cols = slice(hh * 128, (hh + 1) * 128)
            y = yacc[rows, cols] + dpar_ref[0, hp * 2 + hh] * x_ref[rows, cols].astype(F32)
            y_ref[rows, cols] = (y * _silu(z_ref[rows, cols].astype(F32))).astype(BF)
        return carry

    lax.fori_loop(0, T // 256, epi, 0)


def _ssd(xbc, z, dt_bias, a_log, dpar, h0, *, nseq, T, row0):
    smem = pl.BlockSpec(memory_space=pltpu.SMEM)
    return pl.pallas_call(
        functools.partial(_ssd_kernel, T=T),
        grid=(nseq, 4),
        in_specs=[smem, smem, smem,
                  pl.BlockSpec((T, 256), lambda s, p: (s, p)),
                  pl.BlockSpec((T, 128), lambda s, p: (s, 8 + p // 2)),
                  pl.BlockSpec((T, 128), lambda s, p: (s, 10 + p // 2)),
                  pl.BlockSpec((T, 256), lambda s, p: (row0 + s, Z_BZ // 256 + p)),
                  pl.BlockSpec((T, 128), lambda s, p: (row0 + s, Z_SM // 128)),
                  pl.BlockSpec((1, 2, 2, 128, 128), lambda s, p: (s, 0, p, 0, 0))],
        out_specs=[pl.BlockSpec((T, 256), lambda s, p: (s, p)),
                   pl.BlockSpec((1, 2, 2, 128, 128), lambda s, p: (s, 0, p, 0, 0))],
        out_shape=[jax.ShapeDtypeStruct((nseq * T, 1024), BF),
                   jax.ShapeDtypeStruct((nseq, 2, 8, 128, 128), F32)],
        scratch_shapes=[pltpu.VMEM((T, 256), F32), pltpu.VMEM((2, 2, T, 128), F32), pltpu.VMEM((2, 2, T, 128), F32),
                        pltpu.VMEM((2, 2, 128, 128), F32)],
        compiler_params=_cp("arbitrary", "arbitrary"),
        name=f"ssd_scan_T{T}",
    )(dt_bias, a_log, dpar, xbc, xbc, xbc, z, z, h0)


def _rms_rows_kernel(x_ref, g_ref, o_ref):
    x = x_ref[...].astype(F32)
    o_ref[...] = (x * lax.rsqrt(jnp.mean(x * x, axis=-1, keepdims=True) + EPS) * g_ref[...]).astype(BF)


def _rms_rows(x, g):
    tm = 512
    n, w = x.shape
    return pl.pallas_call(
        _rms_rows_kernel,
        grid=(n // tm,),
        in_specs=[pl.BlockSpec((tm, w), lambda i: (i, 0)), pl.BlockSpec((1, w), lambda i: (0, 0))],
        out_specs=pl.BlockSpec((tm, w), lambda i: (i, 0)),
        out_shape=jax.ShapeDtypeStruct((n, w), BF),
        compiler_params=_cp("arbitrary"),
        name="ssd_out_norm",
    )(x, g.reshape(1, w))


def _rope_tables():
    t = np.arange(T_LAT)
    row = (t // GRID_W).astype(np.float32)
    col = (t % GRID_W).astype(np.float32)
    freqs = (ROPE_THETA ** (-np.arange(16, dtype=np.float32) / 16)).astype(np.float32)
    ang_r = row[:, None] * freqs[None, :]
    ang_c = col[:, None] * freqs[None, :]
    ang = np.concatenate([ang_r, ang_r, ang_c, ang_c], axis=-1).astype(np.float32)
    ang = np.concatenate([ang, ang], axis=-1)
    return jnp.cos(jnp.asarray(ang)), jnp.sin(jnp.asarray(ang))


def _rope(x, cos, sin):
    lane = lax.broadcasted_iota(jnp.int32, x.shape, 1)
    even_quarter = ((lane // 16) % 2) == 0
    partner = jnp.where(even_quarter, -pltpu.roll(x, 112, 1), pltpu.roll(x, 16, 1))
    return x * cos + partner * sin


def _mla_kv_kernel(src_ref, sm_ref, cos_ref, sin_ref, wkvb_ref, gkva_ref, gkn_ref, gkp_ref,
                   kh_ref, vh_ref, ckv_ref, *, norm_kv, rope):
    cc = src_ref[...].astype(F32)
    if norm_kv:
        cc = cc * lax.rsqrt(jnp.mean(cc * cc, axis=-1, keepdims=True) + EPS) * gkva_ref[...]
    ckv_ref[...] = cc
    kv = jnp.dot(cc.astype(BF), wkvb_ref[...], preferred_element_type=F32)
    sm = sm_ref[...].astype(F32)
    lane = lax.broadcasted_iota(jnp.int32, sm.shape, 1)
    pe = jnp.where(lane < 64, pltpu.roll(sm, 64, 1), 0.0)
    pe2 = jnp.sum(pe * pe, axis=-1, keepdims=True)
    for h in range(8):
        kn = kv[:, h * 256:h * 256 + 128]
        ri = lax.rsqrt((jnp.sum(kn * kn, axis=-1, keepdims=True) + pe2) * (1.0 / 192.0) + EPS)
        kh_ref[:, h * 256:h * 256 + 128] = (kn * ri * gkn_ref[...]).astype(BF)
        p = pe * ri * gkp_ref[...]
        if rope:
            p = _rope(p, cos_ref[...], sin_ref[...])
        kh_ref[:, h * 256 + 128:(h + 1) * 256] = p.astype(BF)
        vh_ref[:, h * 128:(h + 1) * 128] = kv[:, h * 256 + 128:(h + 1) * 256].astype(BF)


def _mla_kv(src, src_blk, sm, sm_blk, cos, sin, wkvb, gkva, gkn, gkp, *, rows, norm_kv, rope, name):
    tm = 256
    ntab = T_LAT // tm
    return pl.pallas_call(
        functools.partial(_mla_kv_kernel, norm_kv=norm_kv, rope=rope),
        grid=(rows // tm,),
        in_specs=[pl.BlockSpec((tm, 256), lambda i: (src_blk[0] + i, src_blk[1])),
                  pl.BlockSpec((tm, 128), lambda i: (sm_blk[0] + i, sm_blk[1])),
                  pl.BlockSpec((tm, 128), lambda i: (i % ntab, 0)),
                  pl.BlockSpec((tm, 128), lambda i: (i % ntab, 0)),
                  pl.BlockSpec((256, 2048), lambda i: (0, 0)),
                  pl.BlockSpec((1, 256), lambda i: (0, 0)),
                  pl.BlockSpec((1, 128), lambda i: (0, 0)),
                  pl.BlockSpec((1, 128), lambda i: (0, 0))],
        out_specs=[pl.BlockSpec((tm, 2048), lambda i: (i, 0)),
                   pl.BlockSpec((tm, 1024), lambda i: (i, 0)),
                   pl.BlockSpec((tm, 256), lambda i: (i, 0))],
        out_shape=[jax.ShapeDtypeStruct((rows, 2048), BF),
                   jax.ShapeDtypeStruct((rows, 1024), BF),
                   jax.ShapeDtypeStruct((rows, 256), F32)],
        compiler_params=_cp("arbitrary"),
        name=name,
    )(src, sm, cos, sin, wkvb, gkva, gkn, gkp)


def _mla_q_kernel(qa_ref, cos_ref, sin_ref, wqb_ref, gqa_ref, gq_ref, qh_ref, *, rope):
    qa = qa_ref[...].astype(F32)
    qa = qa * lax.rsqrt(jnp.mean(qa * qa, axis=-1, keepdims=True) + EPS) * gqa_ref[...]
    q = jnp.dot(qa.astype(BF), wqb_ref[...], preferred_element_type=F32)
    scale = 192.0 ** -0.5 * LOG2E
    for h in range(8):
        qn = q[:, h * 256:h * 256 + 128]
        qp = q[:, h * 256 + 128:(h + 1) * 256]
        ss = jnp.sum(qn * qn, axis=-1, keepdims=True) + jnp.sum(qp * qp, axis=-1, keepdims=True)
        ri = lax.rsqrt(ss * (1.0 / 192.0) + EPS)
        qh_ref[:, h * 256:h * 256 + 128] = (qn * ri * gq_ref[:, 0:128] * scale).astype(BF)
        p = qp * ri * gq_ref[:, 128:256]
        if rope:
            p = _rope(p, cos_ref[...], sin_ref[...])
        qh_ref[:, h * 256 + 128:(h + 1) * 256] = (p * scale).astype(BF)


def _mla_q(z, row_blk0, cos, sin, wqb, gqa, gq, *, rows, rope, name):
    tm = 256
    ntab = T_LAT // tm
    return pl.pallas_call(
        functools.partial(_mla_q_kernel, rope=rope),
        grid=(rows // tm,),
        in_specs=[pl.BlockSpec((tm, 768), lambda i: (row_blk0 + i, Z_QA // 768)),
                  pl.BlockSpec((tm, 128), lambda i: (i % ntab, 0)),
                  pl.BlockSpec((tm, 128), lambda i: (i % ntab, 0)),
                  pl.BlockSpec((768, 2048), lambda i: (0, 0)),
                  pl.BlockSpec((1, 768), lambda i: (0, 0)),
                  pl.BlockSpec((1, 256), lambda i: (0, 0))],
        out_specs=pl.BlockSpec((tm, 2048), lambda i: (i, 0)),
        out_shape=jax.ShapeDtypeStruct((rows, 2048), BF),
        compiler_params=_cp("arbitrary"),
        name=name,
    )(z, cos, sin, wqb, gqa, gq)


ATTN_ROWS = 256


def _mla_attn_kernel(q_ref, k_ref, v_ref, o_ref):
    for part in range(q_ref.shape[0] // ATTN_ROWS):
        rows = slice(part * ATTN_ROWS, (part + 1) * ATTN_ROWS)
        s = lax.dot_general(q_ref[rows, :], k_ref[...], NT, preferred_element_type=F32)
        p = jnp.exp2(s - jnp.max(s, axis=-1, keepdims=True))
        l = jnp.sum(p, axis=-1, keepdims=True)
        o = jnp.dot(p.astype(BF), v_ref[...], preferred_element_type=F32)
        o_ref[rows, :] = (o / l).astype(BF)


def _mla_attn(q, k, v, *, nb, tq_total, tk, name):
    tq = min(tq_total, 4 * ATTN_ROWS)
    nq = tq_total // tq
    return pl.pallas_call(
        _mla_attn_kernel,
        grid=(nb, 8, nq),
        in_specs=[pl.BlockSpec((tq, 256), lambda b, h, i: (b * nq + i, h)),
                  pl.BlockSpec((tk, 256), lambda b, h, i: (b, h)),
                  pl.BlockSpec((tk, 128), lambda b, h, i: (b, h))],
        out_specs=pl.BlockSpec((tq, 128), lambda b, h, i: (b * nq + i, h)),
        out_shape=jax.ShapeDtypeStruct((nb * tq_total, 1024), BF),
        compiler_params=_cp("arbitrary", "arbitrary", "arbitrary"),
        name=name,
    )(q, k, v)


def _group64_norm(x, g):
    r = lax.broadcasted_iota(jnp.int32, (128, 128), 0)
    c = lax.broadcasted_iota(jnp.int32, (128, 128), 1)
    gm = jnp.where((r // 64) == (c // 64), 1.0, 0.0).astype(BF)
    outs = []
    for j in range(8):
        xj = x[:, j * 128:(j + 1) * 128]
        sq = xj * xj
        hi = sq.astype(BF)
        lo = (sq - hi.astype(F32)).astype(BF)
        ms = (jnp.dot(hi, gm, preferred_element_type=F32) + jnp.dot(lo, gm, preferred_element_type=F32)) * (1.0 / 64.0)
        outs.append(xj * lax.rsqrt(ms + EPS) * g)
    return outs


def _diff_prep_kernel(q_ref, k_ref, cos_ref, sin_ref, gq_ref, gk_ref, qh_ref, kh_ref, kc_ref, *, rope):
    qs = _group64_norm(q_ref[...].astype(F32), gq_ref[...])
    ks = _group64_norm(k_ref[...].astype(F32), gk_ref[...])
    scale = 64.0 ** -0.5 * LOG2E
    for j in range(8):
        cols = slice(j * 128, (j + 1) * 128)
        qj, kj = qs[j], ks[j]
        kc_ref[:, cols] = kj
        if rope:
            qj = _rope(qj, cos_ref[...], sin_ref[...])
            kj = _rope(kj, cos_ref[...], sin_ref[...])
        qh_ref[:, cols] = (qj * scale).astype(BF)
        kh_ref[:, cols] = kj.astype(BF)


def _diff_prep(z, row_blk0, cos, sin, gq, gk, *, rows, rope, name):
    tm = 256
    ntab = T_LAT // tm
    return pl.pallas_call(
        functools.partial(_diff_prep_kernel, rope=rope),
        grid=(rows // tm,),
        in_specs=[pl.BlockSpec((tm, 1024), lambda i: (row_blk0 + i, Z_DQ // 1024)),
                  pl.BlockSpec((tm, 1024), lambda i: (row_blk0 + i, Z_DK // 1024)),
                  pl.BlockSpec((tm, 128), lambda i: (i % ntab, 0)),
                  pl.BlockSpec((tm, 128), lambda i: (i % ntab, 0)),
                  pl.BlockSpec((1, 128), lambda i: (0, 0)),
                  pl.BlockSpec((1, 128), lambda i: (0, 0))],
        out_specs=[pl.BlockSpec((tm, 1024), lambda i: (i, 0)),
                   pl.BlockSpec((tm, 1024), lambda i: (i, 0)),
                   pl.BlockSpec((tm, 1024), lambda i: (i, 0))],
        out_shape=[jax.ShapeDtypeStruct((rows, 1024), BF),
                   jax.ShapeDtypeStruct((rows, 1024), BF),
                   jax.ShapeDtypeStruct((rows, 1024), F32)],
        compiler_params=_cp("arbitrary"),
        name=name,
    )(z, z, cos, sin, gq, gk)


def _diff_attn_kernel(q_ref, k_ref, v_ref, lam_ref, g_ref, o_ref, *, lam_init):
    k = k_ref[...]
    v = v_ref[...]
    lam = lam_ref[...]
    lam_full = (jnp.exp(jnp.sum(lam[0:1, :] * lam[1:2, :], axis=-1, keepdims=True))
                - jnp.exp(jnp.sum(lam[2:3, :] * lam[3:4, :], axis=-1, keepdims=True)) + lam_init)
    for part in range(q_ref.shape[0] // ATTN_ROWS):
        rows = slice(part * ATTN_ROWS, (part + 1) * ATTN_ROWS)
        q = q_ref[rows, :]
        lane = lax.broadcasted_iota(jnp.int32, q.shape, 1)
        zero = jnp.zeros_like(q)
        outs = []
        for m in range(2):
            qm = jnp.where((lane < 64) if m == 0 else (lane >= 64), q, zero)
            s = lax.dot_general(qm, k, NT, preferred_element_type=F32)
            p = jnp.exp2(s - jnp.max(s, axis=-1, keepdims=True))
            l = jnp.sum(p, axis=-1, keepdims=True)
            outs.append(jnp.dot(p.astype(BF), v, preferred_element_type=F32) / l)
        od = outs[0] - lam_full * outs[1]
        y = od * lax.rsqrt(jnp.mean(od * od, axis=-1, keepdims=True) + EPS) * g_ref[...]
        o_ref[rows, :] = (y * (1.0 - lam_init)).astype(BF)


def _diff_attn(q, k, v, v_col0, lam, g, *, nb, tq_total, tk, lam_init, name):
    tq = min(tq_total, 4 * ATTN_ROWS)
    nq = tq_total // tq
    return pl.pallas_call(
        functools.partial(_diff_attn_kernel, lam_init=lam_init),
        grid=(nb, 8, nq),
        in_specs=[pl.BlockSpec((tq, 128), lambda b, h, i: (b * nq + i, h)),
                  pl.BlockSpec((tk, 128), lambda b, h, i: (b, h)),
                  pl.BlockSpec((tk, 128), lambda b, h, i: (b, v_col0 // 128 + h)),
                  pl.BlockSpec((4, 64), lambda b, h, i: (0, 0)),
                  pl.BlockSpec((1, 128), lambda b, h, i: (0, 0))],
        out_specs=pl.BlockSpec((tq, 128), lambda b, h, i: (b * nq + i, h)),
        out_shape=jax.ShapeDtypeStruct((nb * tq_total, 1024), BF),
        compiler_params=_cp("arbitrary", "arbitrary", "arbitrary"),
        name=name,
    )(q, k, v, lam, g)


PEER_TOPK = 16
_CAND_GROUPS = [(0, 16), (1, 8), (2, 5), (3, 4), (4, 3), (5, 2), (6, 2), (7, 2)]


def _top16_rows(s, payload=None):
    nrow = s.shape[0]
    row = lax.broadcasted_iota(jnp.int32, s.shape, 0)
    vals, idxs, pays = [], [], [[] for _ in (payload or [])]
    for _ in range(PEER_TOPK):
        m = jnp.max(s, axis=0, keepdims=True)
        i = jnp.min(jnp.where(s == m, row, nrow), axis=0, keepdims=True)
        hit = row == i
        vals.append(m)
        idxs.append(i)
        for a, pay in enumerate(payload or []):
            pays[a].append(jnp.max(jnp.where(hit, pay, -1), axis=0, keepdims=True))
        s = jnp.where(hit, -jnp.inf, s)
    return (jnp.concatenate(vals, axis=0), jnp.concatenate(idxs, axis=0),
            [jnp.concatenate(p, axis=0) for p in pays])


def _peer_route_kernel(q_ref, keys_ref, w_ref, a_s, b_s, g_s, *, tn):
    sub = lax.broadcasted_iota(jnp.int32, (8, tn), 0)
    for h in range(8):
        tops = []
        for p in range(2):
            qhp = q_ref[:, (h * 2 + p) * 128:(h * 2 + p + 1) * 128]
            st = lax.dot_general(keys_ref[h, p], qhp, NT, preferred_element_type=F32)
            v, i, _ = _top16_rows(st)
            tops.append((v, i))
        (s1, i1), (s2, i2) = tops
        cs, ca, cb = [], [], []
        for r, nvalid in _CAND_GROUPS:
            width = 16 if r == 0 else 8
            blk = s1[r:r + 1, :] + s2[0:width, :]
            if nvalid < width:
                blk = jnp.where(sub < nvalid, blk, -jnp.inf)
            cs.append(blk)
            ca.append(jnp.broadcast_to(i1[r:r + 1, :], (width, tn)))
            cb.append(i2[0:width, :])
        cs.append(s1[8:16, :] + s2[0:1, :])
        ca.append(i1[8:16, :])
        cb.append(jnp.broadcast_to(i2[0:1, :], (8, tn)))
        cand_s = jnp.concatenate(cs, axis=0)
        cand_a = jnp.concatenate(ca, axis=0)
        cand_b = jnp.concatenate(cb, axis=0)
        top_s, _, (sel_a, sel_b) = _top16_rows(cand_s, [cand_a, cand_b])
        e = jnp.exp(top_s - top_s[0:1, :])
        g = e / jnp.sum(e, axis=0, keepdims=True)
        a_s[:, h * 16:(h + 1) * 16] = sel_a.astype(F32).T
        b_s[:, h * 16:(h + 1) * 16] = sel_b.astype(F32).T
        g_s[:, h * 16:(h + 1) * 16] = g.T

    ids = lax.broadcasted_iota(jnp.int32, (128, 128), 0).astype(F32)

    def build(t, carry):
        n0 = pl.multiple_of(t * 16, 16)
        a_rows = a_s[pl.ds(n0, 16), :]
        b_rows = b_s[pl.ds(n0, 16), :]
        g_rows = g_s[pl.ds(n0, 16), :]
        ws = []
        for r in range(16):
            pt = jnp.where(ids == a_rows[r:r + 1, :], g_rows[r:r + 1, :], 0.0).astype(BF)
            qt = jnp.where(ids == b_rows[r:r + 1, :], 1.0, 0.0).astype(BF)
            ws.append(lax.dot_general(pt, qt, NT, preferred_element_type=F32))
        w_ref[:, pl.ds(n0, 16), :] = jnp.swapaxes(jnp.stack(ws, axis=0), 0, 1).astype(BF)
        return carry

    lax.fori_loop(0, tn // 16, build, 0)


def _peer_route(q, keys, layer):
    tn = 128
    n = q.shape[0]
    return pl.pallas_call(
        functools.partial(_peer_route_kernel, tn=tn),
        grid=(n // tn,),
        in_specs=[pl.BlockSpec((tn, 2048), lambda i: (i, 0)),
                  pl.BlockSpec((None, 8, 2, 128, 128), lambda i: (layer, 0, 0, 0, 0))],
        out_specs=pl.BlockSpec((128, tn, 128), lambda i: (0, i, 0)),
        out_shape=jax.ShapeDtypeStruct((128, n, 128), BF),
        scratch_shapes=[pltpu.VMEM((tn, 128), F32), pltpu.VMEM((tn, 128), F32), pltpu.VMEM((tn, 128), F32)],
        compiler_params=_cp("arbitrary"),
        name="peer_route",
    )(q, keys)


def _gelu_tanh(x):
    return 0.5 * x * (1.0 + jnp.tanh(0.7978845608028654 * (x + 0.044715 * x * x * x)))


def _peer_dense_kernel(x_ref, u_ref, v_ref, w_ref, h_ref, mod_ref, o_ref):
    @pl.when(pl.program_id(1) == 0)
    def _():
        o_ref[...] = jnp.zeros_like(o_ref)

    hid = lax.dot_general(x_ref[...], u_ref[...], NT, preferred_element_type=F32)
    w = jnp.concatenate([w_ref[a] for a in range(w_ref.shape[0])], axis=1)
    act = (_gelu_tanh(hid) * w.astype(F32)).astype(BF)
    o_ref[...] += jnp.dot(act, v_ref[...], preferred_element_type=F32)

    @pl.when(pl.program_id(1) == pl.num_programs(1) - 1)
    def _():
        o_ref[...] = h_ref[...] + mod_ref[0][5:6, :] * o_ref[...]


def _peer_dense(x, u, v, w, layer, h, mod3):
    tn, te = 512, 512
    n = x.shape[0]
    ne = u.shape[1]
    return pl.pallas_call(
        _peer_dense_kernel,
        grid=(n // tn, ne // te),
        in_specs=[pl.BlockSpec((tn, D), lambda i, e: (i, 0), pipeline_mode=pl.Buffered(1)),
                  pl.BlockSpec((None, te, D), lambda i, e: (layer, e, 0)),
                  pl.BlockSpec((None, te, D), lambda i, e: (layer, e, 0)),
                  pl.BlockSpec((te // 128, tn, 128), lambda i, e: (e, i, 0)),
                  pl.BlockSpec((tn, D), lambda i, e: (i, 0), pipeline_mode=pl.Buffered(1)),
                  pl.BlockSpec((1, 6, D), lambda i, e: (i * tn // MOD_ROWS, 0, 0))],
        out_specs=pl.BlockSpec((tn, D), lambda i, e: (i, 0), pipeline_mode=pl.Buffered(1)),
        out_shape=jax.ShapeDtypeStruct((n, D), F32),
        compiler_params=_cp("arbitrary", "arbitrary"),
        name="peer_dense",
    )(x, u, v, w, h, mod3)


def _permute_w_in(w):
    o = dict(a_q=0, a_k=512, a_v=1024, a_g=2048, a_gk=3072, b_z=3104, b_xbc=4128, b_dt=5664,
             c_qa=5680, c_kva=6448, c_kpe=6704, d_q=6768, d_k=7792, d_v=8816)

    def seg(name, width):
        return w[..., o[name]:o[name] + width]

    zeros = lambda n: jnp.zeros(w.shape[:-1] + (n,), w.dtype)
    parts = [seg('a_v', 1024), seg('a_g', 1024), seg('b_z', 1024), seg('d_q', 1024), seg('d_k', 1024),
             seg('d_v', 1024), seg('b_xbc', 1536), seg('a_q', 512), seg('a_k', 512), seg('c_kva', 256),
             seg('a_gk', 32), seg('b_dt', 16), zeros(16), seg('c_kpe', 64), zeros(128), seg('c_qa', 768)]
    return jnp.concatenate(parts, axis=-1).astype(BF)


def _permute_w_qb(w):
    w3 = w.reshape(w.shape[0], 8, 192)
    return jnp.concatenate([w3, jnp.zeros((w.shape[0], 8, 64), w.dtype)], axis=-1).reshape(w.shape[0], 2048).astype(BF)


def kernel(x_prompt, x_sample, cache_mla_ckv, cache_mla_kpe, cache_diff_k, cache_diff_v, state_gla, state_ssd, c, c_ctx, norm1_g, norm2_g, w_ada, b_ada, w_in, w_out, gla_w_gk, gla_b_gk, gla_norm_g, ssd_conv_w, ssd_conv_b, ssd_a_log, ssd_dt_bias, ssd_d, ssd_norm_g, mla_qa_norm_g, mla_w_qb, mla_kva_norm_g, mla_w_kvb, mla_q_norm_g, mla_k_norm_g, diff_q_norm_g, diff_k_norm_g, diff_lambda, diff_subln_g, peer_w_q, peer_sub_keys, peer_u, peer_v):
    depth = w_in.shape[0]
    w_in_b = _permute_w_in(w_in)
    w_out_b = w_out.astype(BF)
    peer_wq_b = peer_w_q.astype(BF)
    peer_keys_b = peer_sub_keys.astype(BF)
    peer_u_b = peer_u.astype(BF)
    peer_v_b = peer_v.astype(BF)
    x = jnp.concatenate([x_prompt.reshape(N_CTX, D), x_sample.reshape(N_LAT, D)], axis=0)
    cv8 = jnp.concatenate([c_ctx[None, :], c, jnp.zeros((5, D), F32)], axis=0)
    cos_t, sin_t = _rope_tables()
    nct = N_CTX // T_CTX
    outs = {k: [] for k in ('ckv', 'kpe', 'dk', 'dv', 'gla', 'ssd')}

    for l in range(depth):
        mod3 = _modulation(cv8, w_ada, b_ada, l).reshape(8, 6, D)[:3]

        (z,) = _norm_mm(x, norm1_g[l], mod3, w_in_b, l, shift_row=0, tn=768, emit_xn=False, name="norm_in_proj")

        wgk = gla_w_gk[l]
        wgk_pad = jnp.zeros((2, 128, 512), F32)
        wgk_pad = wgk_pad.at[0, SM_GK:SM_GK + 16].set(wgk[0]).at[1, SM_GK + 16:SM_GK + 32].set(wgk[1]).astype(BF)
        bgk = gla_b_gk[l].reshape(2, 1, 512)
        ng = gla_norm_g[l].reshape(1, 128)
        a_ctx, gla_fin = _gla(z, wgk_pad, bgk, ng, jnp.zeros((B_CTX, 2, 8, 64, 128), F32),
                              nseq=B_CTX, T=T_CTX, row0=0)
        a_lat, _ = _gla(z, wgk_pad, bgk, ng, state_gla[:, l], nseq=B_LAT, T=T_LAT, row0=N_CTX // T_LAT)
        out_a = jnp.concatenate([a_ctx, a_lat], axis=0)

        conv_b = ssd_conv_b[l].reshape(1, 1536)
        xbc_ctx = _conv(z, ssd_conv_w[l], conv_b, nseq=B_CTX, T=T_CTX, row0=0)
        xbc_lat = _conv(z, ssd_conv_w[l], conv_b, nseq=B_LAT, T=T_LAT, row0=N_CTX // T_LAT)
        dpar = ssd_d[l].reshape(1, 8)
        y_ctx, ssd_fin = _ssd(xbc_ctx, z, ssd_dt_bias[l], ssd_a_log[l], dpar,
                              jnp.zeros((B_CTX, 2, 8, 128, 128), F32), nseq=B_CTX, T=T_CTX, row0=0)
        y_lat, _ = _ssd(xbc_lat, z, ssd_dt_bias[l], ssd_a_log[l], dpar, state_ssd[:, l],
                        nseq=B_LAT, T=T_LAT, row0=N_CTX // T_LAT)
        out_b = _rms_rows(jnp.concatenate([y_ctx, y_lat], axis=0), ssd_norm_g[l])

        wkvb = mla_w_kvb[l].astype(BF)
        wqb = _permute_w_qb(mla_w_qb[l])
        gkva = mla_kva_norm_g[l].reshape(1, 256)
        gk = mla_k_norm_g[l]
        gkn = gk[:128].reshape(1, 128)
        gkp = jnp.concatenate([gk[128:], jnp.zeros((64,), F32)]).reshape(1, 128)
        gq = jnp.concatenate([mla_q_norm_g[l], jnp.zeros((64,), F32)]).reshape(1, 256)
        gqa = mla_qa_norm_g[l].reshape(1, 768)
        kh_ctx, vh_ctx, ckv_ctx = _mla_kv(z, (0, Z_KVA // 256), z, (0, Z_SM // 128), cos_t, sin_t, wkvb, gkva, gkn, gkp,
                                          rows=N_CTX, norm_kv=True, rope=False, name="mla_kv_ctx")
        kh_lat, vh_lat, _ = _mla_kv(z, (N_CTX // 256, Z_KVA // 256), z, (N_CTX // 256, Z_SM // 128), cos_t, sin_t,
                                    wkvb, gkva, gkn, gkp, rows=N_LAT, norm_kv=True, rope=True, name="mla_kv_lat")
        cache_sm = jnp.concatenate([jnp.zeros((B_LAT * PAST, 64), F32),
                                    cache_mla_kpe[:, l].reshape(B_LAT * PAST, 64)], axis=1)
        kh_past, vh_past, _ = _mla_kv(cache_mla_ckv[:, l].reshape(B_LAT * PAST, 256), (0, 0), cache_sm, (0, 0),
                                      cos_t, sin_t, wkvb, gkva, gkn, gkp,
                                      rows=B_LAT * PAST, norm_kv=False, rope=False, name="mla_kv_past")
        qh_ctx = _mla_q(z, 0, cos_t, sin_t, wqb, gqa, gq, rows=N_CTX, rope=False, name="mla_q_ctx")
        qh_lat = _mla_q(z, N_CTX // 256, cos_t, sin_t, wqb, gqa, gq, rows=N_LAT, rope=True, name="mla_q_lat")
        tk_lat = PAST + T_LAT
        k_lat = jnp.concatenate([kh_past.reshape(B_LAT, PAST, 2048), kh_lat.reshape(B_LAT, T_LAT, 2048)],
                                axis=1).reshape(B_LAT * tk_lat, 2048)
        v_lat = jnp.concatenate([vh_past.reshape(B_LAT, PAST, 1024), vh_lat.reshape(B_LAT, T_LAT, 1024)],
                                axis=1).reshape(B_LAT * tk_lat, 1024)
        c_ctx_o = _mla_attn(qh_ctx, kh_ctx, vh_ctx, nb=B_CTX, tq_total=T_CTX, tk=T_CTX, name="mla_attn_ctx")
        c_lat_o = _mla_attn(qh_lat, k_lat, v_lat, nb=B_LAT, tq_total=T_LAT, tk=tk_lat, name="mla_attn_lat")
        out_c = jnp.concatenate([c_ctx_o, c_lat_o], axis=0)

        gdq = jnp.tile(diff_q_norm_g[l], 2).reshape(1, 128)
        gdk = jnp.tile(diff_k_norm_g[l], 2).reshape(1, 128)
        dq_ctx, dk_ctx, dkc_ctx = _diff_prep(z, 0, cos_t, sin_t, gdq, gdk, rows=N_CTX, rope=False, name="diff_prep_ctx")
        dq_lat, dk_lat, _ = _diff_prep(z, N_CTX // 256, cos_t, sin_t, gdq, gdk, rows=N_LAT, rope=True,
                                       name="diff_prep_lat")
        dk_full = jnp.concatenate([cache_diff_k[:, l].reshape(B_LAT, PAST, 1024).astype(BF),
                                   dk_lat.reshape(B_LAT, T_LAT, 1024)], axis=1).reshape(B_LAT * tk_lat, 1024)
        dv_full = jnp.concatenate([cache_diff_v[:, l].reshape(B_LAT, PAST, 1024).astype(BF),
                                   z[N_CTX:, Z_DV:Z_DV + 1024].reshape(B_LAT, T_LAT, 1024)],
                                  axis=1).reshape(B_LAT * tk_lat, 1024)
        lam_init = 0.8 - 0.6 * math.exp(-0.3 * l)
        gsub = diff_subln_g[l].reshape(1, 128)
        d_ctx_o = _diff_attn(dq_ctx, dk_ctx, z, Z_DV, diff_lambda[l], gsub, nb=B_CTX, tq_total=T_CTX, tk=T_CTX,
                             lam_init=lam_init, name="diff_attn_ctx")
        d_lat_o = _diff_attn(dq_lat, dk_full, dv_full, 0, diff_lambda[l], gsub, nb=B_LAT, tq_total=T_LAT, tk=tk_lat,
                             lam_init=lam_init, name="diff_attn_lat")
        out_d = jnp.concatenate([d_ctx_o, d_lat_o], axis=0)

        hmid = _out_proj((out_a, out_b, out_c, out_d), w_out_b, l, x, mod3)

        pq, u2 = _norm_mm(hmid, norm2_g[l], mod3, peer_wq_b, l, shift_row=3, tn=512, emit_xn=True,
                          name="norm_peer_query")
        wmap = _peer_route(pq, peer_keys_b, l)
        x = _peer_dense(u2, peer_u_b, peer_v_b, wmap, l, hmid, mod3)

        outs['ckv'].append(ckv_ctx.reshape(B_CTX, T_CTX, 256))
        outs['kpe'].append(z[:N_CTX, Z_SM + SM_KPE:Z_SM + 128].astype(F32).reshape(B_CTX, T_CTX, 64))
        outs['dk'].append(dkc_ctx.reshape(B_CTX, T_CTX, 8, 2, 64))
        outs['dv'].append(z[:N_CTX, Z_DV:Z_DV + 1024].astype(F32).reshape(B_CTX, T_CTX, 8, 128))
        outs['gla'].append(gla_fin)
        outs['ssd'].append(ssd_fin)

    y_p = x[:N_CTX].reshape(B_CTX, T_CTX, D)
    y_s = x[N_CTX:].reshape(B_LAT, T_LAT, D)
    return (y_p, y_s, jnp.stack(outs['ckv'], axis=1), jnp.stack(outs['kpe'], axis=1),
            jnp.stack(outs['dk'], axis=1), jnp.stack(outs['dv'], axis=1),
            jnp.stack(outs['gla'], axis=1), jnp.stack(outs['ssd'], axis=1))
```
